```python
import math
import jax, jax.numpy as jnp
from jax import lax
import numpy as np

D_MODEL = 1024
BATCH = 16
SEQ = 2048
DEPTH = 1

N_Q_HEADS = 8
N_KV_HEADS = 2
HEAD_DIM = 64
WINDOW = 128
BLOCK = 128
ATT_WIDTH = N_Q_HEADS * HEAD_DIM
KV_WIDTH = N_KV_HEADS * HEAD_DIM
GLA_HEADS = 4
GLA_DK = 64
GLA_DV = 128
GLA_K_WIDTH = GLA_HEADS * GLA_DK
GLA_V_WIDTH = GLA_HEADS * GLA_DV
GLA_GATE_RANK = 16
GLA_TAU = 16.0
GLA_CHUNK = 64
REL_BUCKETS = 32
REL_MAX_DIST = 128
D_FF = 2816
EPS = 1e-6
IN_SPLITS = (ATT_WIDTH, KV_WIDTH, KV_WIDTH, GLA_K_WIDTH, GLA_K_WIDTH, GLA_V_WIDTH, GLA_V_WIDTH, GLA_GATE_RANK, D_MODEL, D_MODEL)
IN_WIDTH = sum(IN_SPLITS)

kernel_name = "hybrid_swa_sink_gla_macaron_block"


def rmsnorm(x, g):
    x32 = x.astype(jnp.float32)
    y = x32 * lax.rsqrt(jnp.mean(x32 * x32, axis=-1, keepdims=True) + EPS)
    return (y * g.astype(jnp.float32)).astype(x.dtype)


def swiglu(x, w_gate, w_up, w_down):
    return (jax.nn.silu(x @ w_gate) * (x @ w_up)) @ w_down


def t5_causal_bucket(dist):
    n = jnp.clip(dist, 0, REL_MAX_DIST - 1)
    max_exact = REL_BUCKETS // 2
    nf = jnp.maximum(n, 1).astype(jnp.float32)
    large = max_exact + (jnp.log(nf / max_exact) / math.log(REL_MAX_DIST / max_exact)
                         * (REL_BUCKETS - max_exact)).astype(jnp.int32)
    large = jnp.minimum(large, REL_BUCKETS - 1)
    return jnp.where(n < max_exact, n, large)


def sliding_window_attention(q, k, v, sinks, rel_table):
    B, S = q.shape[0], q.shape[1]
    nb = S // BLOCK
    G = N_Q_HEADS // N_KV_HEADS
    qb = q.reshape(B, nb, BLOCK, N_KV_HEADS, G, HEAD_DIM)

    def band(t):
        tb = t.reshape(B, nb, BLOCK, N_KV_HEADS, HEAD_DIM)
        prev = jnp.pad(tb[:, :-1], ((0, 0), (1, 0), (0, 0), (0, 0), (0, 0)))
        return jnp.concatenate([prev, tb], axis=2)

    kb, vb = band(k), band(v)
    logits = jnp.einsum('bnqkgd,bnskd->bkgnqs', qb, kb).astype(jnp.float32) * (HEAD_DIM ** -0.5)
    t_idx = jnp.arange(BLOCK)[:, None]
    j_idx = jnp.arange(2 * BLOCK)[None, :]
    dist = t_idx + BLOCK - j_idx
    bias = rel_table.astype(jnp.float32)[t5_causal_bucket(dist)]
    bias = bias.transpose(2, 0, 1).reshape(N_KV_HEADS, G, 1, BLOCK, 2 * BLOCK)
    key_pos = (jnp.arange(nb)[:, None, None] - 1) * BLOCK + j_idx[None]
    valid = (dist >= 0)[None] & (dist < WINDOW)[None] & (key_pos >= 0)
    logits = jnp.where(valid, logits + bias, -1e30)
    s = sinks.astype(jnp.float32).reshape(N_KV_HEADS, G, 1, 1, 1)
    m = jnp.maximum(jnp.max(logits, axis=-1, keepdims=True), s)
    p = jnp.exp(logits - m)
    denom = jnp.sum(p, axis=-1, keepdims=True) + jnp.exp(s - m)
    probs = (p / denom).astype(v.dtype)
    out = jnp.einsum('bkgnqs,bnskd->bnqkgd', probs, vb)
    return out.reshape(B, S, ATT_WIDTH)


def gated_linear_attention(q, k, v, log_a):
    B, S = q.shape[0], q.shape[1]
    nc = S // GLA_CHUNK

    def chunks(t):
        return t.reshape(B, nc, GLA_CHUNK, GLA_HEADS, -1).transpose(0, 3, 1, 2, 4).astype(jnp.float32)

    qc = chunks(q) * (GLA_DK ** -0.5)
    kc, vc, la = chunks(k), chunks(v), chunks(log_a)
    b = jnp.cumsum(la, axis=3)
    b_last = b[:, :, :, -1:, :]
    q_dec = qc * jnp.exp(b)
    k_intra = kc * jnp.exp(-b)
    k_state = kc * jnp.exp(b_last - b)
    causal = jnp.tril(jnp.ones((GLA_CHUNK, GLA_CHUNK), dtype=bool))
    attn = jnp.where(causal, jnp.einsum('bhntd,bhnsd->bhnts', q_dec, k_intra), 0.0)
    o_intra = jnp.einsum('bhnts,bhnse->bhnte', attn, vc)
    d_state = jnp.einsum('bhnsd,bhnse->bhnde', k_state, vc)
    decay = jnp.exp(b_last[:, :, :, 0, :])

    def step(state, inp):
        d, ds = inp
        return d[..., None] * state + ds, state

    s0 = jnp.zeros((B, GLA_HEADS, GLA_DK, GLA_DV), jnp.float32)
    _, s_prev = lax.scan(step, s0, (jnp.moveaxis(decay, 2, 0), jnp.moveaxis(d_state, 2, 0)))
    s_prev = jnp.moveaxis(s_prev, 0, 2)
    o = o_intra + jnp.einsum('bhntd,bhnde->bhnte', q_dec, s_prev)
    return o.transpose(0, 2, 3, 1, 4).reshape(B, S, GLA_HEADS, GLA_DV)


def token_mixing(h, rel_table, w_in, w_alpha, b_alpha, attn_sinks, gla_norm_g, w_proj_a, w_proj_b, w_out):
    B, S = h.shape[0], h.shape[1]
    proj = h @ w_in
    cuts = list(np.cumsum(IN_SPLITS)[:-1])
    q_a, k_a, v_a, q_b, k_b, v_b, r_b, a_lr, g_a, g_b = jnp.split(proj, cuts, axis=-1)
    y_a = sliding_window_attention(q_a.reshape(B, S, N_Q_HEADS, HEAD_DIM),
                                   k_a.reshape(B, S, N_KV_HEADS, HEAD_DIM),
                                   v_a.reshape(B, S, N_KV_HEADS, HEAD_DIM),
                                   attn_sinks, rel_table)
    log_a = jax.nn.log_sigmoid((a_lr @ w_alpha + b_alpha).astype(jnp.float32)) / GLA_TAU
    o_b = gated_linear_attention(q_b.reshape(B, S, GLA_HEADS, GLA_DK),
                                 k_b.reshape(B, S, GLA_HEADS, GLA_DK),
                                 v_b.reshape(B, S, GLA_HEADS, GLA_DV),
                                 log_a.reshape(B, S, GLA_HEADS, GLA_DK))
    o_b = rmsnorm(o_b, gla_norm_g).astype(h.dtype) * jax.nn.silu(r_b.reshape(B, S, GLA_HEADS, GLA_DV))
    y_b = o_b.reshape(B, S, GLA_V_WIDTH)
    merged = jax.nn.sigmoid(g_a) * (y_a.astype(h.dtype) @ w_proj_a) + jax.nn.sigmoid(g_b) * (y_b @ w_proj_b)
    return merged @ w_out


def setup_inputs(seed: int = 0) -> dict:
    key = jax.random.key(seed)
    ks = iter(jax.random.split(key, 32))
    f32 = jnp.float32

    def w(shape, fan_in):
        return jax.random.normal(next(ks), shape, f32) * (fan_in ** -0.5)

    def gain(shape):
        return 1.0 + 0.02 * jax.random.normal(next(ks), shape, f32)

    L = DEPTH
    return {
        "x": jax.random.normal(next(ks), (BATCH, SEQ, D_MODEL), f32),
        "rel_bias": 0.1 * jax.random.normal(next(ks), (REL_BUCKETS, N_Q_HEADS), f32),
        "ffn1_pre_g": gain((L, D_MODEL)),
        "ffn1_w_gate": w((L, D_MODEL, D_FF), D_MODEL),
        "ffn1_w_up": w((L, D_MODEL, D_FF), D_MODEL),
        "ffn1_w_down": w((L, D_FF, D_MODEL), D_FF),
        "ffn1_post_g": gain((L, D_MODEL)),
        "mix_pre_g": gain((L, D_MODEL)),
        "w_in": w((L, D_MODEL, IN_WIDTH), D_MODEL),
        "w_alpha": w((L, GLA_GATE_RANK, GLA_K_WIDTH), GLA_GATE_RANK),
        "b_alpha": 0.1 * jax.random.normal(next(ks), (L, GLA_K_WIDTH), f32),
        "attn_sinks": 0.5 * jax.random.normal(next(ks), (L, N_Q_HEADS), f32),
        "gla_norm_g": gain((L, GLA_DV)),
        "w_proj_a": w((L, ATT_WIDTH, D_MODEL), ATT_WIDTH),
        "w_proj_b": w((L, GLA_V_WIDTH, D_MODEL), GLA_V_WIDTH),
        "w_out": w((L, D_MODEL, D_MODEL), D_MODEL),
        "mix_post_g": gain((L, D_MODEL)),
        "ffn2_pre_g": gain((L, D_MODEL)),
        "ffn2_w_gate": w((L, D_MODEL, D_FF), D_MODEL),
        "ffn2_w_up": w((L, D_MODEL, D_FF), D_MODEL),
        "ffn2_w_down": w((L, D_FF, D_MODEL), D_FF),
        "ffn2_post_g": gain((L, D_MODEL)),
    }


def reference(x, rel_bias, ffn1_pre_g, ffn1_w_gate, ffn1_w_up, ffn1_w_down, ffn1_post_g,
              mix_pre_g, w_in, w_alpha, b_alpha, attn_sinks, gla_norm_g, w_proj_a, w_proj_b,
              w_out, mix_post_g, ffn2_pre_g, ffn2_w_gate, ffn2_w_up, ffn2_w_down, ffn2_post_g):
    for l in range(DEPTH):
        f = swiglu(rmsnorm(x, ffn1_pre_g[l]), ffn1_w_gate[l], ffn1_w_up[l], ffn1_w_down[l])
        x = x + 0.5 * rmsnorm(f, ffn1_post_g[l])
        m = token_mixing(rmsnorm(x, mix_pre_g[l]), rel_bias, w_in[l], w_alpha[l], b_alpha[l],
                         attn_sinks[l], gla_norm_g[l], w_proj_a[l], w_proj_b[l], w_out[l])
        x = x + rmsnorm(m, mix_post_g[l])
        f = swiglu(rmsnorm(x, ffn2_pre_g[l]), ffn2_w_gate[l], ffn2_w_up[l], ffn2_w_down[l])
        x = x + 0.5 * rmsnorm(f, ffn2_post_g[l])
    return x
```

```python
import functools
import math

import numpy as np
import jax
import jax.numpy as jnp
from jax import lax
from jax.experimental import pallas as pl
from jax.experimental.pallas import tpu as pltpu

D_MODEL = 1024
D_FF = 2816
EPS = 1e-6

N_Q_HEADS = 8
N_KV_HEADS = 2
HEAD_DIM = 64
WINDOW = 128
BLOCK = 128
ATT_WIDTH = N_Q_HEADS * HEAD_DIM
KV_WIDTH = N_KV_HEADS * HEAD_DIM
KV2_WIDTH = 2 * KV_WIDTH
REL_BUCKETS = 32
REL_MAX_DIST = 128
NEG_INF = -1e30
GLA_HEADS = 4
GLA_DK = 64
GLA_DV = 128
GLA_K_WIDTH = GLA_HEADS * GLA_DK
GLA_V_WIDTH = GLA_HEADS * GLA_DV
GLA_GATE_RANK = 16
GLA_TAU = 16.0
GLA_CHUNK = 64
ALR_PAD = 128
IN_SPLITS = (ATT_WIDTH, KV_WIDTH, KV_WIDTH, GLA_K_WIDTH, GLA_K_WIDTH, GLA_V_WIDTH,
             GLA_V_WIDTH, GLA_GATE_RANK, D_MODEL, D_MODEL)
MIX_ROWS = 256

FF_CHUNK = 256
N_FF_CHUNKS = D_FF // FF_CHUNK
FFN_ROWS = 512

VMEM_LIMIT_BYTES = 56 * 1024 * 1024

F32 = jnp.float32
BF16 = jnp.bfloat16


def _rmsnorm(x, g):
    ms = jnp.mean(x * x, axis=-1, keepdims=True)
    return x * lax.rsqrt(ms + EPS) * g


def _sigmoid(x):
    return 1.0 / (1.0 + jnp.exp(-x))


def _const_spec(shape):
    return pl.BlockSpec(shape, lambda *_: (0,) * len(shape),
                        pipeline_mode=pl.Buffered(1))


def _ffn_kernel(x_ref, pre_g_ref, wgu_ref, wd_ref, post_g_ref, o_ref):
    x = x_ref[...]
    xn = _rmsnorm(x, pre_g_ref[...]).astype(BF16)
    acc = jnp.zeros((x.shape[0], D_MODEL), F32)
    for c in range(N_FF_CHUNKS):
        gu = jnp.dot(xn, wgu_ref[:, c * 2 * FF_CHUNK:(c + 1) * 2 * FF_CHUNK],
                     preferred_element_type=F32)
        g = gu[:, :FF_CHUNK]
        u = gu[:, FF_CHUNK:]
        h = (g * _sigmoid(g) * u).astype(BF16)
        acc = acc + jnp.dot(h, wd_ref[c * FF_CHUNK:(c + 1) * FF_CHUNK, :],
                            preferred_element_type=F32)
    o_ref[...] = x + 0.5 * _rmsnorm(acc, post_g_ref[...])


def _ffn(x2d, pre_g, w_gate, w_up, w_down, post_g):
    n_rows = x2d.shape[0]
    wgu = jnp.concatenate(
        [w_gate.reshape(D_MODEL, N_FF_CHUNKS, FF_CHUNK),
         w_up.reshape(D_MODEL, N_FF_CHUNKS, FF_CHUNK)], axis=-1,
    ).reshape(D_MODEL, 2 * D_FF).astype(BF16)
    wd = w_down.astype(BF16)
    return pl.pallas_call(
        _ffn_kernel,
        out_shape=jax.ShapeDtypeStruct((n_rows, D_MODEL), F32),
        grid=(n_rows // FFN_ROWS,),
        in_specs=[
            pl.BlockSpec((FFN_ROWS, D_MODEL), lambda i: (i, 0)),
            _const_spec((1, D_MODEL)),
            _const_spec((D_MODEL, 2 * D_FF)),
            _const_spec((D_FF, D_MODEL)),
            _const_spec((1, D_MODEL)),
        ],
        out_specs=pl.BlockSpec((FFN_ROWS, D_MODEL), lambda i: (i, 0)),
        compiler_params=pltpu.CompilerParams(
            dimension_semantics=("arbitrary",),
            vmem_limit_bytes=VMEM_LIMIT_BYTES),
        name="ffn",
    )(x2d, pre_g.reshape(1, D_MODEL), wgu, wd, post_g.reshape(1, D_MODEL))


def _dot_nt(a, b):
    return lax.dot_general(a, b, (((1,), (1,)), ((), ())), preferred_element_type=F32)


def _dot_tn(a, b):
    return lax.dot_general(a, b, (((0,), (0,)), ((), ())), preferred_element_type=F32)


def _cumsum_rows(x):
    n = x.shape[0]
    row = lax.broadcasted_iota(jnp.int32, x.shape, 0)
    shift = 1
    while shift < n:
        x = x + jnp.where(row >= shift, pltpu.roll(x, shift, 0), 0.0)
        shift *= 2
    return x


def _log_sigmoid(z):
    return jnp.minimum(z, 0.0) - jnp.log1p(jnp.exp(-jnp.abs(z)))


def _block_mask(shape, row_block, col_block):
    row = lax.broadcasted_iota(jnp.int32, shape, 0) // row_block
    col = lax.broadcasted_iota(jnp.int32, shape, 1) // col_block
    return row == col


def _mixer_kernel(rel_ref, sinks_ref, x_ref, pre_g_ref, w_att_ref, w_gla_ref, w_alr_ref,
                  w_alpha_ref, b_alpha_ref, w_gate_ref, gla_g_ref, w_pa_ref, w_pb_ref,
                  w_out_ref, post_g_ref, bucket_ref, o_ref,
                  kv_scr, st_scr, bias_scr, q_scr, ya_scr, yb_scr):
    rows = x_ref.shape[0]
    seq_step = pl.program_id(1)
    first_of_seq = seq_step == 0

    @pl.when(jnp.logical_and(pl.program_id(0) == 0, first_of_seq))
    def _():
        bucket = bucket_ref[...]
        shape = (BLOCK, 2 * BLOCK)
        dist = (lax.broadcasted_iota(jnp.int32, shape, 0) + BLOCK
                - lax.broadcasted_iota(jnp.int32, shape, 1))
        valid = jnp.logical_and(dist >= 0, dist < WINDOW)
        for hd in range(N_Q_HEADS):
            def body(bk, acc, hd=hd):
                return jnp.where(bucket == bk, rel_ref[bk, hd], acc)
            bias = lax.fori_loop(0, REL_BUCKETS, body, jnp.zeros(shape, F32))
            bias_scr[hd] = jnp.where(valid, bias, NEG_INF)

    @pl.when(first_of_seq)
    def _():
        kv_scr[0:BLOCK, :] = jnp.zeros((BLOCK, kv_scr.shape[1]), BF16)
        st_scr[...] = jnp.zeros(st_scr.shape, F32)

    @pl.when(jnp.logical_not(first_of_seq))
    def _():
        kv_scr[0:BLOCK, :] = kv_scr[rows:rows + BLOCK, :]

    x = x_ref[...]
    h = _rmsnorm(x, pre_g_ref[...]).astype(BF16)

    att = jnp.dot(h, w_att_ref[...], preferred_element_type=F32)
    q_scr[...] = att[:, :ATT_WIDTH].astype(BF16)
    kv_scr[BLOCK:, :] = att[:, ATT_WIDTH:].astype(BF16)

    pair_shape = (BLOCK, 2 * HEAD_DIM)
    low_half = lax.broadcasted_iota(jnp.int32, pair_shape, 1) < HEAD_DIM
    band_col = lax.broadcasted_iota(jnp.int32, (BLOCK, 2 * BLOCK), 1)
    no_prev_block = jnp.logical_and(first_of_seq, band_col < BLOCK)
    for n in range(rows // BLOCK):
        r0 = n * BLOCK
        for pair in range(N_Q_HEADS // 2):
            kvh = pair // (N_Q_HEADS // N_KV_HEADS // 2)
            c0 = pair * 2 * HEAD_DIM
            q_pair = q_scr[r0:r0 + BLOCK, c0:c0 + 2 * HEAD_DIM]
            k2 = kv_scr[r0:r0 + 2 * BLOCK, kvh * 2 * HEAD_DIM:(kvh + 1) * 2 * HEAD_DIM]
            v2 = kv_scr[r0:r0 + 2 * BLOCK,
                        KV2_WIDTH + kvh * 2 * HEAD_DIM:KV2_WIDTH + (kvh + 1) * 2 * HEAD_DIM]
            outs = []
            for e in range(2):
                hd = 2 * pair + e
                sel = low_half if e == 0 else jnp.logical_not(low_half)
                qm = jnp.where(sel, q_pair, jnp.zeros_like(q_pair))
                s = _dot_nt(qm, k2)
                bm = bias_scr[hd]
                if n == 0:
                    bm = jnp.where(no_prev_block, NEG_INF, bm)
                s = s + bm
                sink = sinks_ref[hd]
                m = jnp.maximum(jnp.max(s, axis=-1, keepdims=True), sink)
                p = jnp.exp(s - m)
                denom = jnp.sum(p, axis=-1, keepdims=True) + jnp.exp(sink - m)
                outs.append(jnp.dot(p.astype(BF16), v2, preferred_element_type=F32) / denom)
            ya_scr[r0:r0 + BLOCK, c0:c0 + 2 * HEAD_DIM] = (
                jnp.where(low_half, outs[0], outs[1]).astype(BF16))

    gla = jnp.dot(h, w_gla_ref[...], preferred_element_type=F32)
    a_lr = jnp.dot(h, w_alr_ref[...], preferred_element_type=F32).astype(BF16)
    z = jnp.dot(a_lr, w_alpha_ref[...], preferred_element_type=F32) + b_alpha_ref[...]
    log_a = _log_sigmoid(z) / GLA_TAU

    kbd_mask = _block_mask((GLA_K_WIDTH, GLA_K_WIDTH), GLA_CHUNK, GLA_DK)
    vbd_mask = _block_mask((GLA_HEADS * GLA_CHUNK, GLA_V_WIDTH), GLA_CHUNK, GLA_DV)
    st_mask = _block_mask((GLA_V_WIDTH, GLA_K_WIDTH), GLA_DV, GLA_DK)
    causal = (lax.broadcasted_iota(jnp.int32, (GLA_CHUNK, GLA_K_WIDTH), 0)
              >= lax.broadcasted_iota(jnp.int32, (GLA_CHUNK, GLA_K_WIDTH), 1) % GLA_CHUNK)
    gla_g = gla_g_ref[...]
    for c in range(rows // GLA_CHUNK):
        t0 = c * GLA_CHUNK
        q_c = gla[t0:t0 + GLA_CHUNK, 0:GLA_K_WIDTH]
        k_c = gla[t0:t0 + GLA_CHUNK, GLA_K_WIDTH:2 * GLA_K_WIDTH]
        v_c = gla[t0:t0 + GLA_CHUNK, 2 * GLA_K_WIDTH:2 * GLA_K_WIDTH + GLA_V_WIDTH].astype(BF16)
        r_c = gla[t0:t0 + GLA_CHUNK, 2 * GLA_K_WIDTH + GLA_V_WIDTH:]
        bcum = _cumsum_rows(log_a[t0:t0 + GLA_CHUNK, :])
        b_last = bcum[GLA_CHUNK - 1:GLA_CHUNK, :]
        q_dec = (q_c * jnp.exp(bcum)).astype(BF16)
        k_intra = (k_c * jnp.exp(-bcum)).astype(BF16)
        k_state = (k_c * jnp.exp(b_last - bcum)).astype(BF16)
        decay = jnp.exp(b_last)
        k_bd = jnp.where(kbd_mask, jnp.concatenate([k_intra] * GLA_HEADS, axis=0),
                         jnp.zeros((), BF16))
        v_bd = jnp.where(vbd_mask, jnp.concatenate([v_c] * GLA_HEADS, axis=0),
                         jnp.zeros((), BF16))
        attn = jnp.where(causal, _dot_nt(q_dec, k_bd), 0.0).astype(BF16)
        state_t = st_scr[...]
        o = (jnp.dot(attn, v_bd, preferred_element_type=F32)
             + _dot_nt(q_dec, state_t.astype(BF16)))
        d_state_t = _dot_tn(v_c, k_state)
        st_scr[...] = state_t * decay + jnp.where(st_mask, d_state_t, 0.0)
        parts = []
        for hh in range(GLA_HEADS):
            o_h = o[:, hh * GLA_DV:(hh + 1) * GLA_DV]
            r_h = r_c[:, hh * GLA_DV:(hh + 1) * GLA_DV]
            parts.append(_rmsnorm(o_h, gla_g) * (r_h * _sigmoid(r_h)))
        yb_scr[t0:t0 + GLA_CHUNK, :] = jnp.concatenate(parts, axis=-1).astype(BF16)

    gate = jnp.dot(h, w_gate_ref[...], preferred_element_type=F32)
    m_a = jnp.dot(ya_scr[...], w_pa_ref[...], preferred_element_type=F32)
    m_b = jnp.dot(yb_scr[...], w_pb_ref[...], preferred_element_type=F32)
    merged = (_sigmoid(gate[:, :D_MODEL]) * m_a + _sigmoid(gate[:, D_MODEL:]) * m_b).astype(BF16)
    mixed = jnp.dot(merged, w_out_ref[...], preferred_element_type=F32)
    o_ref[...] = x + _rmsnorm(mixed, post_g_ref[...])


def _t5_bucket_table():
    t = np.arange(BLOCK)[:, None]
    j = np.arange(2 * BLOCK)[None, :]
    n = np.clip(t + BLOCK - j, 0, REL_MAX_DIST - 1)
    max_exact = REL_BUCKETS // 2
    nf = np.maximum(n, 1).astype(np.float32)
    large = max_exact + (np.log(nf / np.float32(max_exact))
                         / np.float32(math.log(REL_MAX_DIST / max_exact))
                         * np.float32(REL_BUCKETS - max_exact)).astype(np.int32)
    large = np.minimum(large, REL_BUCKETS - 1)
    return np.where(n < max_exact, n, large).astype(np.int32)


def _mixer(x2d, seq_len, rel_bias, pre_g, w_in, w_alpha, b_alpha, sinks, gla_g,
           w_proj_a, w_proj_b, w_out, post_g):
    n_rows = x2d.shape[0]
    cuts = np.cumsum(IN_SPLITS)[:-1]
    w_q, w_k, w_v, w_qb, w_kb, w_vb, w_rb, w_alr, w_ga, w_gb = jnp.split(w_in, cuts, axis=1)

    def dup_heads(w):
        w = w.reshape(D_MODEL, N_KV_HEADS, 1, HEAD_DIM)
        return jnp.broadcast_to(w, (D_MODEL, N_KV_HEADS, 2, HEAD_DIM)).reshape(D_MODEL, KV2_WIDTH)

    w_att = jnp.concatenate([w_q * (HEAD_DIM ** -0.5), dup_heads(w_k), dup_heads(w_v)],
                            axis=1).astype(BF16)
    w_gla = jnp.concatenate([w_qb * (GLA_DK ** -0.5), w_kb, w_vb, w_rb], axis=1).astype(BF16)
    w_alr_p = jnp.pad(w_alr, ((0, 0), (0, ALR_PAD - GLA_GATE_RANK))).astype(BF16)
    w_alpha_p = jnp.pad(w_alpha, ((0, ALR_PAD - GLA_GATE_RANK), (0, 0))).astype(BF16)
    w_gate = jnp.concatenate([w_ga, w_gb], axis=1).astype(BF16)
    bucket = jnp.asarray(_t5_bucket_table())

    steps = seq_len // MIX_ROWS
    smem = pl.BlockSpec(memory_space=pltpu.SMEM)
    row_spec = pl.BlockSpec((MIX_ROWS, D_MODEL), lambda b, j: (b * steps + j, 0))
    return pl.pallas_call(
        _mixer_kernel,
        out_shape=jax.ShapeDtypeStruct((n_rows, D_MODEL), F32),
        grid=(n_rows // seq_len, steps),
        in_specs=[
            smem, smem, row_spec,
            _const_spec((1, D_MODEL)),
            _const_spec(w_att.shape), _const_spec(w_gla.shape), _const_spec(w_alr_p.shape),
            _const_spec(w_alpha_p.shape), _const_spec((1, GLA_K_WIDTH)),
            _const_spec(w_gate.shape), _const_spec((1, GLA_DV)),
            _const_spec((ATT_WIDTH, D_MODEL)), _const_spec((GLA_V_WIDTH, D_MODEL)),
            _const_spec((D_MODEL, D_MODEL)), _const_spec((1, D_MODEL)),
            _const_spec(bucket.shape),
        ],
        out_specs=row_spec,
        scratch_shapes=[
            pltpu.VMEM((MIX_ROWS + BLOCK, 2 * KV2_WIDTH), BF16),
            pltpu.VMEM((GLA_V_WIDTH, GLA_K_WIDTH), F32),
            pltpu.VMEM((N_Q_HEADS, BLOCK, 2 * BLOCK), F32),
            pltpu.VMEM((MIX_ROWS, ATT_WIDTH), BF16),
            pltpu.VMEM((MIX_ROWS, ATT_WIDTH), BF16),
            pltpu.VMEM((MIX_ROWS, GLA_V_WIDTH), BF16),
        ],
        compiler_params=pltpu.CompilerParams(
            dimension_semantics=("arbitrary", "arbitrary"),
            vmem_limit_bytes=VMEM_LIMIT_BYTES),
        name="mixer",
    )(rel_bias, sinks, x2d, pre_g.reshape(1, D_MODEL), w_att, w_gla, w_alr_p, w_alpha_p,
      b_alpha.reshape(1, GLA_K_WIDTH), w_gate, gla_g.reshape(1, GLA_DV),
      w_proj_a.astype(BF16), w_proj_b.astype(BF16), w_out.astype(BF16),
      post_g.reshape(1, D_MODEL), bucket)


def kernel(x, rel_bias, ffn1_pre_g, ffn1_w_gate, ffn1_w_up, ffn1_w_down, ffn1_post_g, mix_pre_g, w_in, w_alpha, b_alpha, attn_sinks, gla_norm_g, w_proj_a, w_proj_b, w_out, mix_post_g, ffn2_pre_g, ffn2_w_gate, ffn2_w_up, ffn2_w_down, ffn2_post_g):
    b, s, d = x.shape
    x2d = x.reshape(b * s, d)
    x2d = _ffn(x2d, ffn1_pre_g[0], ffn1_w_gate[0], ffn1_w_up[0], ffn1_w_down[0], ffn1_post_g[0])
    x2d = _mixer(x2d, s, rel_bias, mix_pre_g[0], w_in[0], w_alpha[0], b_alpha[0], attn_sinks[0],
                 gla_norm_g[0], w_proj_a[0], w_proj_b[0], w_out[0], mix_post_g[0])
    x2d = _ffn(x2d, ffn2_pre_g[0], ffn2_w_gate[0], ffn2_w_up[0], ffn2_w_down[0], ffn2_post_g[0])
    return x2d.reshape(b, s, d)
```

```python
import functools
import math

import numpy as np
import jax
import jax.numpy as jnp
from jax import lax
from jax.experimental import pallas as pl
from jax.experimental.pallas import tpu as pltpu

D_MODEL = 1024
D_FF = 2816
EPS = 1e-6

N_Q_HEADS = 8
N_KV_HEADS = 2
HEAD_DIM = 64
WINDOW = 128
BLOCK = 128
ATT_WIDTH = N_Q_HEADS * HEAD_DIM
KV_WIDTH = N_KV_HEADS * HEAD_DIM
KV2_WIDTH = 2 * KV_WIDTH
REL_BUCKETS = 32
REL_MAX_DIST = 128
NEG_INF = -1e30
GLA_HEADS = 4
GLA_DK = 64
GLA_DV = 128
GLA_K_WIDTH = GLA_HEADS * GLA_DK
GLA_V_WIDTH = GLA_HEADS * GLA_DV
GLA_GATE_RANK = 16
GLA_TAU = 16.0
GLA_CHUNK = 64
ALR_PAD = 128
IN_SPLITS = (ATT_WIDTH, KV_WIDTH, KV_WIDTH, GLA_K_WIDTH, GLA_K_WIDTH, GLA_V_WIDTH,
             GLA_V_WIDTH, GLA_GATE_RANK, D_MODEL, D_MODEL)
MIX_ROWS = 256

FF_CHUNK = 256
N_FF_CHUNKS = D_FF // FF_CHUNK
FFN_SUB_ROWS = 512
FFN_ROWS = 2 * FFN_SUB_ROWS

VMEM_LIMIT_BYTES = 56 * 1024 * 1024

F32 = jnp.float32
BF16 = jnp.bfloat16


def _rmsnorm(x, g):
    ms = jnp.mean(x * x, axis=-1, keepdims=True)
    return x * lax.rsqrt(ms + EPS) * g


def _sigmoid(x):
    return 1.0 / (1.0 + jnp.exp(-x))


def _const_spec(shape):
    return pl.BlockSpec(shape, lambda *_: (0,) * len(shape),
                        pipeline_mode=pl.Buffered(1))


def _ffn_kernel(x_ref, pre_g_ref, wg_ref, wu_ref, wd_ref, post_g_ref, o_ref):
    starts = range(0, x_ref.shape[0], FFN_SUB_ROWS)
    xn = [_rmsnorm(x_ref[r0:r0 + FFN_SUB_ROWS, :], pre_g_ref[...]).astype(BF16) for r0 in starts]
    acc = [jnp.zeros((FFN_SUB_ROWS, D_MODEL), F32) for _ in starts]
    for c in range(N_FF_CHUNKS):
        cols = slice(c * FF_CHUNK, (c + 1) * FF_CHUNK)
        for t, r0 in enumerate(starts):
            g = jnp.dot(xn[t], wg_ref[:, cols], preferred_element_type=F32)
            u = jnp.dot(xn[t], wu_ref[:, cols], preferred_element_type=F32)
            h = (g * _sigmoid(g) * u).astype(BF16)
            acc[t] = acc[t] + jnp.dot(h, wd_ref[cols, :], preferred_element_type=F32)
            if c == N_FF_CHUNKS - 1:
                o_ref[r0:r0 + FFN_SUB_ROWS, :] = (
                    x_ref[r0:r0 + FFN_SUB_ROWS, :] + 0.5 * _rmsnorm(acc[t], post_g_ref[...]))


def _ffn(x2d, pre_g, w_gate, w_up, w_down, post_g):
    n_rows = x2d.shape[0]
    return pl.pallas_call(
        _ffn_kernel,
        out_shape=jax.ShapeDtypeStruct((n_rows, D_MODEL), F32),
        grid=(n_rows // FFN_ROWS,),
        in_specs=[
            pl.BlockSpec((FFN_ROWS, D_MODEL), lambda i: (i, 0)),
            _const_spec((1, D_MODEL)),
            _const_spec((D_MODEL, D_FF)),
            _const_spec((D_MODEL, D_FF)),
            _const_spec((D_FF, D_MODEL)),
            _const_spec((1, D_MODEL)),
        ],
        out_specs=pl.BlockSpec((FFN_ROWS, D_MODEL), lambda i: (i, 0)),
        compiler_params=pltpu.CompilerParams(
            dimension_semantics=("arbitrary",),
            vmem_limit_bytes=VMEM_LIMIT_BYTES),
        name="ffn",
    )(x2d, pre_g.reshape(1, D_MODEL), w_gate.astype(BF16), w_up.astype(BF16),
      w_down.astype(BF16), post_g.reshape(1, D_MODEL))


def _dot_nt(a, b):
    return lax.dot_general(a, b, (((1,), (1,)), ((), ())), preferred_element_type=F32)


def _dot_tn(a, b):
    return lax.dot_general(a, b, (((0,), (0,)), ((), ())), preferred_element_type=F32)


def _cumsum_rows(x):
    n = x.shape[0]
    row = lax.broadcasted_iota(jnp.int32, x.shape, 0)
    shift = 1
    while shift < n:
        x = x + jnp.where(row >= shift, pltpu.roll(x, shift, 0), 0.0)
        shift *= 2
    return x


def _log_sigmoid(z):
    return jnp.minimum(z, 0.0) - jnp.log1p(jnp.exp(-jnp.abs(z)))


def _block_mask(shape, row_block, col_block):
    row = lax.broadcasted_iota(jnp.int32, shape, 0) // row_block
    col = lax.broadcasted_iota(jnp.int32, shape, 1) // col_block
    return row == col


def _mixer_kernel(rel_ref, sinks_ref, x_ref, pre_g_ref, w_att_ref, w_gla_ref, w_alr_ref,
                  w_alpha_ref, b_alpha_ref, w_gate_ref, gla_g_ref, w_pa_ref, w_pb_ref,
                  w_out_ref, post_g_ref, bucket_ref, o_ref,
                  kv_scr, st_scr, bias_scr, q_scr, ya_scr, yb_scr):
    rows = x_ref.shape[0]
    seq_step = pl.program_id(1)
    first_of_seq = seq_step == 0

    @pl.when(jnp.logical_and(pl.program_id(0) == 0, first_of_seq))
    def _():
        bucket = bucket_ref[...]
        shape = (BLOCK, 2 * BLOCK)
        dist = (lax.broadcasted_iota(jnp.int32, shape, 0) + BLOCK
                - lax.broadcasted_iota(jnp.int32, shape, 1))
        valid = jnp.logical_and(dist >= 0, dist < WINDOW)
        for hd in range(N_Q_HEADS):
            def body(bk, acc, hd=hd):
                return jnp.where(bucket == bk, rel_ref[bk, hd], acc)
            bias = lax.fori_loop(0, REL_BUCKETS, body, jnp.zeros(shape, F32))
            bias_scr[hd] = jnp.where(valid, bias, NEG_INF)

    @pl.when(first_of_seq)
    def _():
        kv_scr[0:BLOCK, :] = jnp.zeros((BLOCK, kv_scr.shape[1]), BF16)
        st_scr[...] = jnp.zeros(st_scr.shape, F32)

    @pl.when(jnp.logical_not(first_of_seq))
    def _():
        kv_scr[0:BLOCK, :] = kv_scr[rows:rows + BLOCK, :]

    x = x_ref[...]
    h = _rmsnorm(x, pre_g_ref[...]).astype(BF16)

    att = jnp.dot(h, w_att_ref[...], preferred_element_type=F32)
    q_scr[...] = att[:, :ATT_WIDTH].astype(BF16)
    kv_scr[BLOCK:, :] = att[:, ATT_WIDTH:].astype(BF16)

    pair_shape = (BLOCK, 2 * HEAD_DIM)
    low_half = lax.broadcasted_iota(jnp.int32, pair_shape, 1) < HEAD_DIM
    band_col = lax.broadcasted_iota(jnp.int32, (BLOCK, 2 * BLOCK), 1)
    no_prev_block = jnp.logical_and(first_of_seq, band_col < BLOCK)
    for n in range(rows // BLOCK):
        r0 = n * BLOCK
        for pair in range(N_Q_HEADS // 2):
            kvh = pair // (N_Q_HEADS // N_KV_HEADS // 2)
            c0 = pair * 2 * HEAD_DIM
            q_pair = q_scr[r0:r0 + BLOCK, c0:c0 + 2 * HEAD_DIM]
            k2 = kv_scr[r0:r0 + 2 * BLOCK, kvh * 2 * HEAD_DIM:(kvh + 1) * 2 * HEAD_DIM]
            v2 = kv_scr[r0:r0 + 2 * BLOCK,
                        KV2_WIDTH + kvh * 2 * HEAD_DIM:KV2_WIDTH + (kvh + 1) * 2 * HEAD_DIM]
            outs = []
            for e in range(2):
                hd = 2 * pair + e
                sel = low_half if e == 0 else jnp.logical_not(low_half)
                qm = jnp.where(sel, q_pair, jnp.zeros_like(q_pair))
                s = _dot_nt(qm, k2)
                bm = bias_scr[hd]
                if n == 0:
                    bm = jnp.where(no_prev_block, NEG_INF, bm)
                s = s + bm
                sink = sinks_ref[hd]
                m = jnp.maximum(jnp.max(s, axis=-1, keepdims=True), sink)
                p = jnp.exp(s - m)
                denom = jnp.sum(p, axis=-1, keepdims=True) + jnp.exp(sink - m)
                outs.append(jnp.dot(p.astype(BF16), v2, preferred_element_type=F32) / denom)
            ya_scr[r0:r0 + BLOCK, c0:c0 + 2 * HEAD_DIM] = (
                jnp.where(low_half, outs[0], outs[1]).astype(BF16))

    gla = jnp.dot(h, w_gla_ref[...], preferred_element_type=F32)
    a_lr = jnp.dot(h, w_alr_ref[...], preferred_element_type=F32).astype(BF16)
    z = jnp.dot(a_lr, w_alpha_ref[...], preferred_element_type=F32) + b_alpha_ref[...]
    log_a = _log_sigmoid(z) / GLA_TAU

    kbd_mask = _block_mask((GLA_K_WIDTH, GLA_K_WIDTH), GLA_CHUNK, GLA_DK)
    vbd_mask = _block_mask((GLA_HEADS * GLA_CHUNK, GLA_V_WIDTH), GLA_CHUNK, GLA_DV)
    st_mask = _block_mask((GLA_V_WIDTH, GLA_K_WIDTH), GLA_DV, GLA_DK)
    causal = (lax.broadcasted_iota(jnp.int32, (GLA_CHUNK, GLA_K_WIDTH), 0)
              >= lax.broadcasted_iota(jnp.int32, (GLA_CHUNK, GLA_K_WIDTH), 1) % GLA_CHUNK)
    gla_g = gla_g_ref[...]
    for c in range(rows // GLA_CHUNK):
        t0 = c * GLA_CHUNK
        q_c = gla[t0:t0 + GLA_CHUNK, 0:GLA_K_WIDTH]
        k_c = gla[t0:t0 + GLA_CHUNK, GLA_K_WIDTH:2 * GLA_K_WIDTH]
        v_c = gla[t0:t0 + GLA_CHUNK, 2 * GLA_K_WIDTH:2 * GLA_K_WIDTH + GLA_V_WIDTH].astype(BF16)
        r_c = gla[t0:t0 + GLA_CHUNK, 2 * GLA_K_WIDTH + GLA_V_WIDTH:]
        bcum = _cumsum_rows(log_a[t0:t0 + GLA_CHUNK, :])
        b_last = bcum[GLA_CHUNK - 1:GLA_CHUNK, :]
        q_dec = (q_c * jnp.exp(bcum)).astype(BF16)
        k_intra = (k_c * jnp.exp(-bcum)).astype(BF16)
        k_state = (k_c * jnp.exp(b_last - bcum)).astype(BF16)
        decay = jnp.exp(b_last)
        k_bd = jnp.where(kbd_mask, jnp.concatenate([k_intra] * GLA_HEADS, axis=0),
                         jnp.zeros((), BF16))
        v_bd = jnp.where(vbd_mask, jnp.concatenate([v_c] * GLA_HEADS, axis=0),
                         jnp.zeros((), BF16))
        attn = jnp.where(causal, _dot_nt(q_dec, k_bd), 0.0).astype(BF16)
        state_t = st_scr[...]
        o = (jnp.dot(attn, v_bd, preferred_element_type=F32)
             + _dot_nt(q_dec, state_t.astype(BF16)))
        d_state_t = _dot_tn(v_c, k_state)
        st_scr[...] = state_t * decay + jnp.where(st_mask, d_state_t, 0.0)
        parts = []
        for hh in range(GLA_HEADS):
            o_h = o[:, hh * GLA_DV:(hh + 1) * GLA_DV]
            r_h = r_c[:, hh * GLA_DV:(hh + 1) * GLA_DV]
            parts.append(_rmsnorm(o_h, gla_g) * (r_h * _sigmoid(r_h)))
        yb_scr[t0:t0 + GLA_CHUNK, :] = jnp.concatenate(parts, axis=-1).astype(BF16)

    gate = jnp.dot(h, w_gate_ref[...], preferred_element_type=F32)
    m_a = jnp.dot(ya_scr[...], w_pa_ref[...], preferred_element_type=F32)
    m_b = jnp.dot(yb_scr[...], w_pb_ref[...], preferred_element_type=F32)
    merged = (_sigmoid(gate[:, :D_MODEL]) * m_a + _sigmoid(gate[:, D_MODEL:]) * m_b).astype(BF16)
    mixed = jnp.dot(merged, w_out_ref[...], preferred_element_type=F32)
    o_ref[...] = x + _rmsnorm(mixed, post_g_ref[...])


def _t5_bucket_table():
    t = np.arange(BLOCK)[:, None]
    j = np.arange(2 * BLOCK)[None, :]
    n = np.clip(t + BLOCK - j, 0, REL_MAX_DIST - 1)
    max_exact = REL_BUCKETS // 2
    nf = np.maximum(n, 1).astype(np.float32)
    large = max_exact + (np.log(nf / np.float32(max_exact))
                         / np.float32(math.log(REL_MAX_DIST / max_exact))
                         * np.float32(REL_BUCKETS - max_exact)).astype(np.int32)
    large = np.minimum(large, REL_BUCKETS - 1)
    return np.where(n < max_exact, n, large).astype(np.int32)


def _mixer(x2d, seq_len, rel_bias, pre_g, w_in, w_alpha, b_alpha, sinks, gla_g,
           w_proj_a, w_proj_b, w_out, post_g):
    n_rows = x2d.shape[0]
    offs = np.concatenate([[0], np.cumsum(IN_SPLITS)])
    (o_q, o_k, o_v, o_qb, o_kb, _, _, o_alr, o_ga, o_end) = (int(offs[i]) for i in
                                                            (0, 1, 2, 3, 4, 5, 6, 7, 8, 10))

    def dup_heads(o):
        return [w_in[:, o + hd * HEAD_DIM:o + (hd + 1) * HEAD_DIM]
                for hd in range(N_KV_HEADS) for _ in range(2)]

    w_att = jnp.concatenate([w_in[:, o_q:o_k] * (HEAD_DIM ** -0.5)] + dup_heads(o_k) + dup_heads(o_v),
                            axis=1).astype(BF16)
    w_gla = jnp.concatenate([w_in[:, o_qb:o_kb] * (GLA_DK ** -0.5), w_in[:, o_kb:o_alr]],
                            axis=1).astype(BF16)
    w_alr_p = jnp.pad(w_in[:, o_alr:o_ga], ((0, 0), (0, ALR_PAD - GLA_GATE_RANK))).astype(BF16)
    w_alpha_p = jnp.pad(w_alpha, ((0, ALR_PAD - GLA_GATE_RANK), (0, 0))).astype(BF16)
    w_gate = w_in[:, o_ga:o_end].astype(BF16)
    bucket = jnp.asarray(_t5_bucket_table())

    steps = seq_len // MIX_ROWS
    smem = pl.BlockSpec(memory_space=pltpu.SMEM)
    row_spec = pl.BlockSpec((MIX_ROWS, D_MODEL), lambda b, j: (b * steps + j, 0))
    return pl.pallas_call(
        _mixer_kernel,
        out_shape=jax.ShapeDtypeStruct((n_rows, D_MODEL), F32),
        grid=(n_rows // seq_len, steps),
        in_specs=[
            smem, smem, row_spec,
            _const_spec((1, D_MODEL)),
            _const_spec(w_att.shape), _const_spec(w_gla.shape), _const_spec(w_alr_p.shape),
            _const_spec(w_alpha_p.shape), _const_spec((1, GLA_K_WIDTH)),
            _const_spec(w_gate.shape), _const_spec((1, GLA_DV)),
            _const_spec((ATT_WIDTH, D_MODEL)), _const_spec((GLA_V_WIDTH, D_MODEL)),
            _const_spec((D_MODEL, D_MODEL)), _const_spec((1, D_MODEL)),
            _const_spec(bucket.shape),
        ],
        out_specs=row_spec,
        scratch_shapes=[
            pltpu.VMEM((MIX_ROWS + BLOCK, 2 * KV2_WIDTH), BF16),
            pltpu.VMEM((GLA_V_WIDTH, GLA_K_WIDTH), F32),
            pltpu.VMEM((N_Q_HEADS, BLOCK, 2 * BLOCK), F32),
            pltpu.VMEM((MIX_ROWS, ATT_WIDTH), BF16),
            pltpu.VMEM((MIX_ROWS, ATT_WIDTH), BF16),
            pltpu.VMEM((MIX_ROWS, GLA_V_WIDTH), BF16),
        ],
        compiler_params=pltpu.CompilerParams(
            dimension_semantics=("arbitrary", "arbitrary"),
            vmem_limit_bytes=VMEM_LIMIT_BYTES),
        name="mixer",
    )(rel_bias, sinks, x2d, pre_g.reshape(1, D_MODEL), w_att, w_gla, w_alr_p, w_alpha_p,
      b_alpha.reshape(1, GLA_K_WIDTH), w_gate, gla_g.reshape(1, GLA_DV),
      w_proj_a.astype(BF16), w_proj_b.astype(BF16), w_out.astype(BF16),
      post_g.reshape(1, D_MODEL), bucket)


def kernel(x, rel_bias, ffn1_pre_g, ffn1_w_gate, ffn1_w_up, ffn1_w_down, ffn1_post_g, mix_pre_g, w_in, w_alpha, b_alpha, attn_sinks, gla_norm_g, w_proj_a, w_proj_b, w_out, mix_post_g, ffn2_pre_g, ffn2_w_gate, ffn2_w_up, ffn2_w_down, ffn2_post_g):
    b, s, d = x.shape
    x2d = x.reshape(b * s, d)
    x2d = _ffn(x2d, ffn1_pre_g[0], ffn1_w_gate[0], ffn1_w_up[0], ffn1_w_down[0], ffn1_post_g[0])
    x2d = _mixer(x2d, s, rel_bias, mix_pre_g[0], w_in[0], w_alpha[0], b_alpha[0], attn_sinks[0],
                 gla_norm_g[0], w_proj_a[0], w_proj_b[0], w_out[0], mix_post_g[0])
    x2d = _ffn(x2d, ffn2_pre_g[0], ffn2_w_gate[0], ffn2_w_up[0], ffn2_w_down[0], ffn2_post_g[0])
    return x2d.reshape(b, s, d)
```

```python
import functools
import math

import numpy as np
import jax
import jax.numpy as jnp
from jax import lax
from jax.experimental import pallas as pl
from jax.experimental.pallas import tpu as pltpu

D_MODEL = 1024
D_FF = 2816
EPS = 1e-6

N_Q_HEADS = 8
N_KV_HEADS = 2
HEAD_DIM = 64
WINDOW = 128
BLOCK = 128
ATT_WIDTH = N_Q_HEADS * HEAD_DIM
KV_WIDTH = N_KV_HEADS * HEAD_DIM
KV2_WIDTH = 2 * KV_WIDTH
REL_BUCKETS = 32
REL_MAX_DIST = 128
NEG_INF = -1e30
GLA_HEADS = 4
GLA_DK = 64
GLA_DV = 128
GLA_K_WIDTH = GLA_HEADS * GLA_DK
GLA_V_WIDTH = GLA_HEADS * GLA_DV
GLA_GATE_RANK = 16
GLA_TAU = 16.0
GLA_CHUNK = 64
ALR_PAD = 128
IN_SPLITS = (ATT_WIDTH, KV_WIDTH, KV_WIDTH, GLA_K_WIDTH, GLA_K_WIDTH, GLA_V_WIDTH,
             GLA_V_WIDTH, GLA_GATE_RANK, D_MODEL, D_MODEL)
MIX_ROWS = 512
TRIL_ROWS = 256
GATE_PIECE_COLS = 512
N_GATE_PIECES = 2 * D_MODEL // GATE_PIECE_COLS

FF_CHUNK = 256
N_FF_CHUNKS = D_FF // FF_CHUNK
FFN_SUB_ROWS = 512
FFN_ROWS = 2 * FFN_SUB_ROWS

VMEM_LIMIT_BYTES = 56 * 1024 * 1024

F32 = jnp.float32
BF16 = jnp.bfloat16


def _rmsnorm(x, g):
    ms = jnp.mean(x * x, axis=-1, keepdims=True)
    return x * lax.rsqrt(ms + EPS) * g


def _sigmoid(x):
    return 1.0 / (1.0 + jnp.exp(-x))


def _const_spec(shape):
    return pl.BlockSpec(shape, lambda *_: (0,) * len(shape),
                        pipeline_mode=pl.Buffered(1))


def _ffn_kernel(x_ref, pre_g_ref, wg_ref, wu_ref, wd_ref, post_g_ref, o_ref):
    starts = range(0, x_ref.shape[0], FFN_SUB_ROWS)
    xn = [_rmsnorm(x_ref[r0:r0 + FFN_SUB_ROWS, :], pre_g_ref[...]).astype(BF16) for r0 in starts]
    acc = [jnp.zeros((FFN_SUB_ROWS, D_MODEL), F32) for _ in starts]
    for c in range(N_FF_CHUNKS):
        cols = slice(c * FF_CHUNK, (c + 1) * FF_CHUNK)
        for t, r0 in enumerate(starts):
            g = jnp.dot(xn[t], wg_ref[:, cols], preferred_element_type=F32)
            u = jnp.dot(xn[t], wu_ref[:, cols], preferred_element_type=F32)
            h = (g * _sigmoid(g) * u).astype(BF16)
            acc[t] = acc[t] + jnp.dot(h, wd_ref[cols, :], preferred_element_type=F32)
            if c == N_FF_CHUNKS - 1:
                o_ref[r0:r0 + FFN_SUB_ROWS, :] = (
                    x_ref[r0:r0 + FFN_SUB_ROWS, :] + 0.5 * _rmsnorm(acc[t], post_g_ref[...]))


def _ffn(x2d, pre_g, w_gate, w_up, w_down, post_g):
    n_rows = x2d.shape[0]
    return pl.pallas_call(
        _ffn_kernel,
        out_shape=jax.ShapeDtypeStruct((n_rows, D_MODEL), F32),
        grid=(n_rows // FFN_ROWS,),
        in_specs=[
            pl.BlockSpec((FFN_ROWS, D_MODEL), lambda i: (i, 0)),
            _const_spec((1, D_MODEL)),
            _const_spec((D_MODEL, D_FF)),
            _const_spec((D_MODEL, D_FF)),
            _const_spec((D_FF, D_MODEL)),
            _const_spec((1, D_MODEL)),
        ],
        out_specs=pl.BlockSpec((FFN_ROWS, D_MODEL), lambda i: (i, 0)),
        compiler_params=pltpu.CompilerParams(
            dimension_semantics=("arbitrary",),
            vmem_limit_bytes=VMEM_LIMIT_BYTES),
        name="ffn",
    )(x2d, pre_g.reshape(1, D_MODEL), w_gate.astype(BF16), w_up.astype(BF16),
      w_down.astype(BF16), post_g.reshape(1, D_MODEL))


def _dot_nt(a, b):
    return lax.dot_general(a, b, (((1,), (1,)), ((), ())), preferred_element_type=F32)


def _log_sigmoid(z):
    return jnp.minimum(z, 0.0) - jnp.log1p(jnp.exp(-jnp.abs(z)))


def _block_mask(shape, row_block, col_block):
    row = lax.broadcasted_iota(jnp.int32, shape, 0) // row_block
    col = lax.broadcasted_iota(jnp.int32, shape, 1) // col_block
    return row == col


def _mixer_kernel(rel_ref, sinks_ref, x_ref, pre_g_ref, w_att_ref, w_gla_ref, w_alr_ref,
                  w_alpha_ref, b_alpha_ref, w_gate_ref, gla_g_ref, w_pa_ref, w_pb_ref,
                  w_out_ref, post_g_ref, bucket_ref, o_ref,
                  kv_scr, st_scr, bias_scr, tril_scr, kbd_scr, vbd_scr,
                  h_scr, q_scr, ya_scr, yb_scr, og_scr, sg_scr):
    rows = x_ref.shape[0]
    seq_step = pl.program_id(1)
    first_of_seq = seq_step == 0

    @pl.when(jnp.logical_and(pl.program_id(0) == 0, first_of_seq))
    def _():
        bucket = bucket_ref[...]
        shape = (BLOCK, 2 * BLOCK)
        dist = (lax.broadcasted_iota(jnp.int32, shape, 0) + BLOCK
                - lax.broadcasted_iota(jnp.int32, shape, 1))
        valid = jnp.logical_and(dist >= 0, dist < WINDOW)
        for hd in range(N_Q_HEADS):
            def body(bk, acc, hd=hd):
                return jnp.where(bucket == bk, rel_ref[bk, hd], acc)
            bias = lax.fori_loop(0, REL_BUCKETS, body, jnp.zeros(shape, F32))
            bias_scr[hd] = jnp.where(valid, bias, NEG_INF)
        tshape = (TRIL_ROWS, TRIL_ROWS)
        trow = lax.broadcasted_iota(jnp.int32, tshape, 0)
        tcol = lax.broadcasted_iota(jnp.int32, tshape, 1)
        tril_scr[...] = jnp.logical_and(trow // GLA_CHUNK == tcol // GLA_CHUNK,
                                        tcol <= trow).astype(F32).astype(BF16)
        kbd_scr[...] = _block_mask(kbd_scr.shape, GLA_CHUNK, GLA_DK).astype(F32).astype(BF16)
        vbd_scr[...] = _block_mask(vbd_scr.shape, GLA_CHUNK, GLA_DV).astype(F32).astype(BF16)

    @pl.when(first_of_seq)
    def _():
        kv_scr[0:BLOCK, :] = jnp.zeros((BLOCK, kv_scr.shape[1]), BF16)
        st_scr[...] = jnp.zeros(st_scr.shape, F32)

    @pl.when(jnp.logical_not(first_of_seq))
    def _():
        kv_scr[0:BLOCK, :] = kv_scr[rows:rows + BLOCK, :]

    half_rows = rows // 2
    halves = [slice(i * half_rows, (i + 1) * half_rows) for i in range(2)]
    for rs in halves:
        h_scr[rs, :] = _rmsnorm(x_ref[rs, :], pre_g_ref[...]).astype(BF16)
    h = h_scr[...]

    def attention_projection(rs):
        att = jnp.dot(h_scr[rs, :], w_att_ref[...], preferred_element_type=F32)
        q_scr[rs, :] = att[:, :ATT_WIDTH].astype(BF16)
        kv_scr[BLOCK + rs.start:BLOCK + rs.stop, :] = att[:, ATT_WIDTH:].astype(BF16)

    gate_pieces = iter(range(N_GATE_PIECES))

    def gate_piece():
        i = next(gate_pieces)
        gc = slice(i * GATE_PIECE_COLS, (i + 1) * GATE_PIECE_COLS)
        sg_scr[:, gc] = _sigmoid(jnp.dot(h, w_gate_ref[:, gc], preferred_element_type=F32))

    a_lr = jnp.dot(h, w_alr_ref[...], preferred_element_type=F32).astype(BF16)
    attention_projection(halves[0])
    z = jnp.dot(a_lr, w_alpha_ref[...], preferred_element_type=F32) + b_alpha_ref[...]
    attention_projection(halves[1])
    log_a = _log_sigmoid(z) / GLA_TAU
    la_hi = log_a.astype(BF16)
    la_lo = (log_a - la_hi.astype(F32)).astype(BF16)
    tril = tril_scr[...]
    bcum = jnp.concatenate(
        [jnp.dot(tril, la_hi[g0:g0 + TRIL_ROWS, :], preferred_element_type=F32)
         + jnp.dot(tril, la_lo[g0:g0 + TRIL_ROWS, :], preferred_element_type=F32)
         for g0 in range(0, rows, TRIL_ROWS)], axis=0)

    pair_shape = (BLOCK, 2 * HEAD_DIM)
    low_half = lax.broadcasted_iota(jnp.int32, pair_shape, 1) < HEAD_DIM
    band_col = lax.broadcasted_iota(jnp.int32, (BLOCK, 2 * BLOCK), 1)
    no_prev_block = jnp.logical_and(first_of_seq, band_col < BLOCK)

    def kv_band(n, kvh, which):
        c0 = which * KV2_WIDTH + kvh * 2 * HEAD_DIM
        return kv_scr[n * BLOCK:(n + 2) * BLOCK, c0:c0 + 2 * HEAD_DIM]

    def attention_scores(n):
        scores = []
        for hd in range(N_Q_HEADS):
            pair = hd // 2
            q_pair = q_scr[n * BLOCK:(n + 1) * BLOCK, pair * 2 * HEAD_DIM:(pair + 1) * 2 * HEAD_DIM]
            sel = low_half if hd % 2 == 0 else jnp.logical_not(low_half)
            qm = jnp.where(sel, q_pair, jnp.zeros_like(q_pair))
            scores.append(_dot_nt(qm, kv_band(n, hd // (N_Q_HEADS // N_KV_HEADS), 0)))
        return scores

    def attention_outputs(n, scores):
        outs = []
        for hd, s in enumerate(scores):
            bm = bias_scr[hd]
            if n == 0:
                bm = jnp.where(no_prev_block, NEG_INF, bm)
            s = s + bm
            sink = sinks_ref[hd]
            m = jnp.maximum(jnp.max(s, axis=-1, keepdims=True), sink)
            p = jnp.exp(s - m)
            denom = jnp.sum(p, axis=-1, keepdims=True) + jnp.exp(sink - m)
            v2 = kv_band(n, hd // (N_Q_HEADS // N_KV_HEADS), 1)
            outs.append(jnp.dot(p.astype(BF16), v2, preferred_element_type=F32) / denom)
        for pair in range(N_Q_HEADS // 2):
            ya_scr[n * BLOCK:(n + 1) * BLOCK, pair * 2 * HEAD_DIM:(pair + 1) * 2 * HEAD_DIM] = (
                jnp.where(low_half, outs[2 * pair], outs[2 * pair + 1]).astype(BF16))

    scores = attention_scores(0)
    qk = jnp.dot(h, w_gla_ref[:, :2 * GLA_K_WIDTH], preferred_element_type=F32)
    attention_outputs(0, scores)
    scores = attention_scores(1)
    v_all = jnp.dot(h, w_gla_ref[:, 2 * GLA_K_WIDTH:2 * GLA_K_WIDTH + GLA_V_WIDTH],
                    preferred_element_type=F32).astype(BF16)
    attention_outputs(1, scores)
    q_all = qk[:, :GLA_K_WIDTH]
    k_all = qk[:, GLA_K_WIDTH:]
    q_dec_all = (q_all * jnp.exp(bcum)).astype(BF16)
    k_intra_all = (k_all * jnp.exp(-bcum)).astype(BF16)
    for n in range(2, rows // BLOCK):
        scores = attention_scores(n)
        gate_piece()
        attention_outputs(n, scores)

    chunks = [slice(c * GLA_CHUNK, (c + 1) * GLA_CHUNK) for c in range(rows // GLA_CHUNK)]
    kbd_mask = kbd_scr[...] != 0
    vbd_mask = vbd_scr[...] != 0
    causal = (lax.broadcasted_iota(jnp.int32, (GLA_CHUNK, GLA_K_WIDTH), 0)
              >= lax.broadcasted_iota(jnp.int32, (GLA_CHUNK, GLA_K_WIDTH), 1) % GLA_CHUNK)
    attn, d_state, decay_col = [], [], []
    for cs in chunks:
        k_bd = jnp.where(kbd_mask, jnp.concatenate([k_intra_all[cs, :]] * GLA_HEADS, axis=0),
                         jnp.zeros((), BF16))
        attn.append(jnp.where(causal, _dot_nt(q_dec_all[cs, :], k_bd), 0.0).astype(BF16))
        b_c = bcum[cs, :]
        b_last = b_c[GLA_CHUNK - 1:GLA_CHUNK, :]
        k_state_t = (k_all[cs, :] * jnp.exp(b_last - b_c)).T.astype(BF16)
        d_state.append([jnp.dot(k_state_t[hh * GLA_DK:(hh + 1) * GLA_DK, :],
                                v_all[cs, hh * GLA_DV:(hh + 1) * GLA_DV],
                                preferred_element_type=F32) for hh in range(GLA_HEADS)])
        decay = jnp.exp(b_last)
        decay_col.append(jnp.concatenate(
            [jnp.broadcast_to(decay[:, l0:l0 + 128], (128, 128)).T
             for l0 in range(0, GLA_K_WIDTH, 128)], axis=0))
    gate_piece()

    zero_blk = jnp.zeros((GLA_DK, GLA_DV), BF16)
    state = [st_scr[hh * GLA_DK:(hh + 1) * GLA_DK, :] for hh in range(GLA_HEADS)]
    for c, cs in enumerate(chunks):
        v_bd = jnp.where(vbd_mask, jnp.concatenate([v_all[cs, :]] * GLA_HEADS, axis=0),
                         jnp.zeros((), BF16))
        s_bd = jnp.concatenate(
            [jnp.concatenate([zero_blk] * hh + [state[hh].astype(BF16)]
                             + [zero_blk] * (GLA_HEADS - 1 - hh), axis=1)
             for hh in range(GLA_HEADS)], axis=0)
        og_scr[cs, :] = (jnp.dot(attn[c], v_bd, preferred_element_type=F32)
                         + jnp.dot(q_dec_all[cs, :], s_bd, preferred_element_type=F32))
        state = [state[hh] * decay_col[c][hh * GLA_DK:(hh + 1) * GLA_DK, :] + d_state[c][hh]
                 for hh in range(GLA_HEADS)]
    for hh in range(GLA_HEADS):
        st_scr[hh * GLA_DK:(hh + 1) * GLA_DK, :] = state[hh]

    r_all = jnp.dot(h, w_gla_ref[:, 2 * GLA_K_WIDTH + GLA_V_WIDTH:], preferred_element_type=F32)
    gate_piece()
    assert next(gate_pieces, None) is None
    m_a = jnp.dot(ya_scr[...], w_pa_ref[...], preferred_element_type=F32)
    gla_g = gla_g_ref[...]
    for hh in range(GLA_HEADS):
        lanes = slice(hh * GLA_DV, (hh + 1) * GLA_DV)
        r_h = r_all[:, lanes]
        yb_scr[:, lanes] = (_rmsnorm(og_scr[:, lanes], gla_g) * (r_h * _sigmoid(r_h))).astype(BF16)

    m_b = [jnp.dot(yb_scr[rs, :], w_pb_ref[...], preferred_element_type=F32) for rs in halves]
    mixed = []
    for i, rs in enumerate(halves):
        merged = (sg_scr[rs, :D_MODEL] * m_a[rs, :] + sg_scr[rs, D_MODEL:] * m_b[i]).astype(BF16)
        mixed.append(jnp.dot(merged, w_out_ref[...], preferred_element_type=F32))
    for i, rs in enumerate(halves):
        o_ref[rs, :] = x_ref[rs, :] + _rmsnorm(mixed[i], post_g_ref[...])


def _t5_bucket_table():
    t = np.arange(BLOCK)[:, None]
    j = np.arange(2 * BLOCK)[None, :]
    n = np.clip(t + BLOCK - j, 0, REL_MAX_DIST - 1)
    max_exact = REL_BUCKETS // 2
    nf = np.maximum(n, 1).astype(np.float32)
    large = max_exact + (np.log(nf / np.float32(max_exact))
                         / np.float32(math.log(REL_MAX_DIST / max_exact))
                         * np.float32(REL_BUCKETS - max_exact)).astype(np.int32)
    large = np.minimum(large, REL_BUCKETS - 1)
    return np.where(n < max_exact, n, large).astype(np.int32)


def _mixer(x2d, seq_len, rel_bias, pre_g, w_in, w_alpha, b_alpha, sinks, gla_g,
           w_proj_a, w_proj_b, w_out, post_g):
    n_rows = x2d.shape[0]
    offs = np.concatenate([[0], np.cumsum(IN_SPLITS)])
    (o_q, o_k, o_v, o_qb, o_kb, _, _, o_alr, o_ga, o_end) = (int(offs[i]) for i in
                                                            (0, 1, 2, 3, 4, 5, 6, 7, 8, 10))

    def dup_heads(o):
        return [w_in[:, o + hd * HEAD_DIM:o + (hd + 1) * HEAD_DIM]
                for hd in range(N_KV_HEADS) for _ in range(2)]

    w_att = jnp.concatenate([w_in[:, o_q:o_k] * (HEAD_DIM ** -0.5)] + dup_heads(o_k) + dup_heads(o_v),
                            axis=1).astype(BF16)
    w_gla = jnp.concatenate([w_in[:, o_qb:o_kb] * (GLA_DK ** -0.5), w_in[:, o_kb:o_alr]],
                            axis=1).astype(BF16)
    w_alr_p = jnp.pad(w_in[:, o_alr:o_ga], ((0, 0), (0, ALR_PAD - GLA_GATE_RANK))).astype(BF16)
    w_alpha_p = jnp.pad(w_alpha, ((0, ALR_PAD - GLA_GATE_RANK), (0, 0))).astype(BF16)
    w_gate = w_in[:, o_ga:o_end].astype(BF16)
    bucket = jnp.asarray(_t5_bucket_table())

    steps = seq_len // MIX_ROWS
    smem = pl.BlockSpec(memory_space=pltpu.SMEM)
    row_spec = pl.BlockSpec((MIX_ROWS, D_MODEL), lambda b, j: (b * steps + j, 0))
    return pl.pallas_call(
        _mixer_kernel,
        out_shape=jax.ShapeDtypeStruct((n_rows, D_MODEL), F32),
        grid=(n_rows // seq_len, steps),
        in_specs=[
            smem, smem, row_spec,
            _const_spec((1, D_MODEL)),
            _const_spec(w_att.shape), _const_spec(w_gla.shape), _const_spec(w_alr_p.shape),
            _const_spec(w_alpha_p.shape), _const_spec((1, GLA_K_WIDTH)),
            _const_spec(w_gate.shape), _const_spec((1, GLA_DV)),
            _const_spec((ATT_WIDTH, D_MODEL)), _const_spec((GLA_V_WIDTH, D_MODEL)),
            _const_spec((D_MODEL, D_MODEL)), _const_spec((1, D_MODEL)),
            _const_spec(bucket.shape),
        ],
        out_specs=row_spec,
        scratch_shapes=[
            pltpu.VMEM((MIX_ROWS + BLOCK, 2 * KV2_WIDTH), BF16),
            pltpu.VMEM((GLA_K_WIDTH, GLA_DV), F32),
            pltpu.VMEM((N_Q_HEADS, BLOCK, 2 * BLOCK), F32),
            pltpu.VMEM((TRIL_ROWS, TRIL_ROWS), BF16),
            pltpu.VMEM((GLA_HEADS * GLA_CHUNK, GLA_K_WIDTH), BF16),
            pltpu.VMEM((GLA_HEADS * GLA_CHUNK, GLA_V_WIDTH), BF16),
            pltpu.VMEM((MIX_ROWS, D_MODEL), BF16),
            pltpu.VMEM((MIX_ROWS, ATT_WIDTH), BF16),
            pltpu.VMEM((MIX_ROWS, ATT_WIDTH), BF16),
            pltpu.VMEM((MIX_ROWS, GLA_V_WIDTH), BF16),
            pltpu.VMEM((MIX_ROWS, GLA_V_WIDTH), F32),
            pltpu.VMEM((MIX_ROWS, 2 * D_MODEL), F32),
        ],
        compiler_params=pltpu.CompilerParams(
            dimension_semantics=("arbitrary", "arbitrary"),
            vmem_limit_bytes=VMEM_LIMIT_BYTES),
        name="mixer",
    )(rel_bias, sinks, x2d, pre_g.reshape(1, D_MODEL), w_att, w_gla, w_alr_p, w_alpha_p,
      b_alpha.reshape(1, GLA_K_WIDTH), w_gate, gla_g.reshape(1, GLA_DV),
      w_proj_a.astype(BF16), w_proj_b.astype(BF16), w_out.astype(BF16),
      post_g.reshape(1, D_MODEL), bucket)


def kernel(x, rel_bias, ffn1_pre_g, ffn1_w_gate, ffn1_w_up, ffn1_w_down, ffn1_post_g, mix_pre_g, w_in, w_alpha, b_alpha, attn_sinks, gla_norm_g, w_proj_a, w_proj_b, w_out, mix_post_g, ffn2_pre_g, ffn2_w_gate, ffn2_w_up, ffn2_w_down, ffn2_post_g):
    b, s, d = x.shape
    x2d = x.reshape(b * s, d)
    x2d = _ffn(x2d, ffn1_pre_g[0], ffn1_w_gate[0], ffn1_w_up[0], ffn1_w_down[0], ffn1_post_g[0])
    x2d = _mixer(x2d, s, rel_bias, mix_pre_g[0], w_in[0], w_alpha[0], b_alpha[0], attn_sinks[0],
                 gla_norm_g[0], w_proj_a[0], w_proj_b[0], w_out[0], mix_post_g[0])
    x2d = _ffn(x2d, ffn2_pre_g[0], ffn2_w_gate[0], ffn2_w_up[0], ffn2_w_down[0], ffn2_post_g[0])
    return x2d.reshape(b, s, d)
```

```python
import functools
import math

import numpy as np
import jax
import jax.numpy as jnp
from jax import lax
from jax.experimental import pallas as pl
from jax.experimental.pallas import tpu as pltpu

D_MODEL = 1024
D_FF = 2816
EPS = 1e-6

N_Q_HEADS = 8
N_KV_HEADS = 2
HEAD_DIM = 64
WINDOW = 128
BLOCK = 128
ATT_WIDTH = N_Q_HEADS * HEAD_DIM
KV_WIDTH = N_KV_HEADS * HEAD_DIM
KV2_WIDTH = 2 * KV_WIDTH
BF16_SUBLANES = 16
REL_BUCKETS = 32
REL_MAX_DIST = 128
NEG_INF = -1e30
GLA_HEADS = 4
GLA_DK = 64
GLA_DV = 128
GLA_K_WIDTH = GLA_HEADS * GLA_DK
GLA_V_WIDTH = GLA_HEADS * GLA_DV
GLA_GATE_RANK = 16
GLA_TAU = 16.0
GLA_CHUNK = 64
ALR_PAD = 128
IN_SPLITS = (ATT_WIDTH, KV_WIDTH, KV_WIDTH, GLA_K_WIDTH, GLA_K_WIDTH, GLA_V_WIDTH,
             GLA_V_WIDTH, GLA_GATE_RANK, D_MODEL, D_MODEL)
MIX_ROWS = 512
TRIL_ROWS = 256
GATE_PIECE_COLS = 512
N_GATE_PIECES = 2 * D_MODEL // GATE_PIECE_COLS

FF_CHUNK = 256
N_FF_CHUNKS = D_FF // FF_CHUNK
FFN_SUB_ROWS = 512
FFN_ROWS = 2 * FFN_SUB_ROWS

VMEM_LIMIT_BYTES = 56 * 1024 * 1024

F32 = jnp.float32
BF16 = jnp.bfloat16


def _rmsnorm(x, g):
    ms = jnp.mean(x * x, axis=-1, keepdims=True)
    return x * lax.rsqrt(ms + EPS) * g


def _sigmoid(x):
    return 1.0 / (1.0 + jnp.exp(-x))


def _const_spec(shape):
    return pl.BlockSpec(shape, lambda *_: (0,) * len(shape),
                        pipeline_mode=pl.Buffered(1))


def _ffn_kernel(x_ref, pre_g_ref, wg_ref, wu_ref, wd_ref, post_g_ref, o_ref):
    starts = range(0, x_ref.shape[0], FFN_SUB_ROWS)
    xn = [_rmsnorm(x_ref[r0:r0 + FFN_SUB_ROWS, :], pre_g_ref[...]).astype(BF16) for r0 in starts]
    acc = [jnp.zeros((FFN_SUB_ROWS, D_MODEL), F32) for _ in starts]
    for c in range(N_FF_CHUNKS):
        cols = slice(c * FF_CHUNK, (c + 1) * FF_CHUNK)
        for t, r0 in enumerate(starts):
            g = jnp.dot(xn[t], wg_ref[:, cols], preferred_element_type=F32)
            u = jnp.dot(xn[t], wu_ref[:, cols], preferred_element_type=F32)
            h = (g * _sigmoid(g) * u).astype(BF16)
            acc[t] = acc[t] + jnp.dot(h, wd_ref[cols, :], preferred_element_type=F32)
            if c == N_FF_CHUNKS - 1:
                o_ref[r0:r0 + FFN_SUB_ROWS, :] = (
                    x_ref[r0:r0 + FFN_SUB_ROWS, :] + 0.5 * _rmsnorm(acc[t], post_g_ref[...]))


def _ffn(x2d, pre_g, w_gate, w_up, w_down, post_g):
    n_rows = x2d.shape[0]
    return pl.pallas_call(
        _ffn_kernel,
        out_shape=jax.ShapeDtypeStruct((n_rows, D_MODEL), F32),
        grid=(n_rows // FFN_ROWS,),
        in_specs=[
            pl.BlockSpec((FFN_ROWS, D_MODEL), lambda i: (i, 0)),
            _const_spec((1, D_MODEL)),
            _const_spec((D_MODEL, D_FF)),
            _const_spec((D_MODEL, D_FF)),
            _const_spec((D_FF, D_MODEL)),
            _const_spec((1, D_MODEL)),
        ],
        out_specs=pl.BlockSpec((FFN_ROWS, D_MODEL), lambda i: (i, 0)),
        compiler_params=pltpu.CompilerParams(
            dimension_semantics=("arbitrary",),
            vmem_limit_bytes=VMEM_LIMIT_BYTES),
        name="ffn",
    )(x2d, pre_g.reshape(1, D_MODEL), w_gate.astype(BF16), w_up.astype(BF16),
      w_down.astype(BF16), post_g.reshape(1, D_MODEL))


def _dot_nt(a, b):
    return lax.dot_general(a, b, (((1,), (1,)), ((), ())), preferred_element_type=F32)


def _log_sigmoid(z):
    return jnp.minimum(z, 0.0) - jnp.log1p(jnp.exp(-jnp.abs(z)))


def _block_mask(shape, row_block, col_block):
    row = lax.broadcasted_iota(jnp.int32, shape, 0) // row_block
    col = lax.broadcasted_iota(jnp.int32, shape, 1) // col_block
    return row == col


def _mixer_kernel(rel_ref, sinks_ref, x_ref, pre_g_ref, w_att_ref, w_gla_ref, w_alr_ref,
                  w_alpha_ref, b_alpha_ref, w_gate_ref, gla_g_ref, w_pa_ref, w_pb_ref,
                  w_out_ref, post_g_ref, bucket_ref, o_ref,
                  k_scr, vx_scr, st_scr, bias_scr, tril_scr, kbd_scr, vbd_scr,
                  h_scr, q_scr, ya_scr, yb_scr, og_scr, sg_scr):
    rows = x_ref.shape[0]
    seq_step = pl.program_id(1)
    first_of_seq = seq_step == 0

    @pl.when(jnp.logical_and(pl.program_id(0) == 0, first_of_seq))
    def _():
        bucket = bucket_ref[...]
        shape = (BLOCK, 2 * BLOCK)
        dist = (lax.broadcasted_iota(jnp.int32, shape, 0) + BLOCK
                - lax.broadcasted_iota(jnp.int32, shape, 1))
        valid = jnp.logical_and(dist >= 0, dist < WINDOW)
        for hd in range(N_Q_HEADS):
            def body(bk, acc, hd=hd):
                return jnp.where(bucket == bk, rel_ref[bk, hd], acc)
            bias = lax.fori_loop(0, REL_BUCKETS, body, jnp.zeros(shape, F32))
            bias_scr[hd] = jnp.where(lax.broadcasted_iota(jnp.int32, shape, 1) == 0,
                                     sinks_ref[hd], jnp.where(valid, bias, NEG_INF))
        ones = jnp.ones((vx_scr.shape[0], 2 * HEAD_DIM), BF16)
        for kvh in range(N_KV_HEADS):
            vx_scr[:, (2 * kvh + 1) * 2 * HEAD_DIM:(2 * kvh + 2) * 2 * HEAD_DIM] = ones
        tshape = (TRIL_ROWS, TRIL_ROWS)
        trow = lax.broadcasted_iota(jnp.int32, tshape, 0)
        tcol = lax.broadcasted_iota(jnp.int32, tshape, 1)
        tril_scr[...] = jnp.logical_and(trow // GLA_CHUNK == tcol // GLA_CHUNK,
                                        tcol <= trow).astype(F32).astype(BF16)
        kbd_scr[...] = _block_mask(kbd_scr.shape, GLA_CHUNK, GLA_DK).astype(F32).astype(BF16)
        vbd_scr[...] = _block_mask(vbd_scr.shape, GLA_CHUNK, GLA_DV).astype(F32).astype(BF16)

    @pl.when(first_of_seq)
    def _():
        k_scr[0:BLOCK, :] = jnp.zeros((BLOCK, k_scr.shape[1]), BF16)
        for kvh in range(N_KV_HEADS):
            vx_scr[0:BLOCK, 2 * kvh * 2 * HEAD_DIM:(2 * kvh + 1) * 2 * HEAD_DIM] = (
                jnp.zeros((BLOCK, 2 * HEAD_DIM), BF16))
        st_scr[...] = jnp.zeros(st_scr.shape, F32)

    @pl.when(jnp.logical_not(first_of_seq))
    def _():
        k_scr[0:BLOCK, :] = k_scr[rows:rows + BLOCK, :]
        vx_scr[0:BLOCK, :] = vx_scr[rows:rows + BLOCK, :]

    half_rows = rows // 2
    halves = [slice(i * half_rows, (i + 1) * half_rows) for i in range(2)]
    for rs in halves:
        h_scr[rs, :] = _rmsnorm(x_ref[rs, :], pre_g_ref[...]).astype(BF16)
    h = h_scr[...]

    def attention_projection(rs):
        att = jnp.dot(h_scr[rs, :], w_att_ref[...], preferred_element_type=F32)
        q_scr[rs, :] = att[:, :ATT_WIDTH].astype(BF16)
        new_rows = slice(BLOCK + rs.start, BLOCK + rs.stop)
        k_scr[new_rows, :] = att[:, ATT_WIDTH:ATT_WIDTH + KV2_WIDTH].astype(BF16)
        for kvh in range(N_KV_HEADS):
            c0 = ATT_WIDTH + KV2_WIDTH + kvh * 2 * HEAD_DIM
            vx_scr[new_rows, 2 * kvh * 2 * HEAD_DIM:(2 * kvh + 1) * 2 * HEAD_DIM] = (
                att[:, c0:c0 + 2 * HEAD_DIM].astype(BF16))

    gate_pieces = iter(range(N_GATE_PIECES))

    def gate_piece():
        i = next(gate_pieces)
        gc = slice(i * GATE_PIECE_COLS, (i + 1) * GATE_PIECE_COLS)
        sg_scr[:, gc] = _sigmoid(jnp.dot(h, w_gate_ref[:, gc], preferred_element_type=F32))

    a_lr = jnp.dot(h, w_alr_ref[...], preferred_element_type=F32).astype(BF16)
    attention_projection(halves[0])
    z = jnp.dot(a_lr, w_alpha_ref[...], preferred_element_type=F32) + b_alpha_ref[...]
    attention_projection(halves[1])
    log_a = _log_sigmoid(z) / GLA_TAU
    la_hi = log_a.astype(BF16)
    la_lo = (log_a - la_hi.astype(F32)).astype(BF16)
    tril = tril_scr[...]
    bcum = jnp.concatenate(
        [jnp.dot(tril, la_hi[g0:g0 + TRIL_ROWS, :], preferred_element_type=F32)
         + jnp.dot(tril, la_lo[g0:g0 + TRIL_ROWS, :], preferred_element_type=F32)
         for g0 in range(0, rows, TRIL_ROWS)], axis=0)

    pair_shape = (BLOCK, 2 * HEAD_DIM)
    low_half = lax.broadcasted_iota(jnp.int32, pair_shape, 1) < HEAD_DIM
    band_col = lax.broadcasted_iota(jnp.int32, (BLOCK, 2 * BLOCK), 1)
    no_prev_block = jnp.logical_and(first_of_seq,
                                    jnp.logical_and(band_col > 0, band_col < BLOCK))
    top = BF16_SUBLANES
    row0 = lax.broadcasted_iota(jnp.int32, (top, 2 * HEAD_DIM), 0) == 0

    def band(ref, n, c0):
        r0 = n * BLOCK
        head = jnp.where(row0, jnp.zeros((), BF16), ref[r0:r0 + top, c0:c0 + 2 * HEAD_DIM])
        return jnp.concatenate([head, ref[r0 + top:r0 + 2 * BLOCK, c0:c0 + 2 * HEAD_DIM]], axis=0)

    def attention_scores(n):
        scores = []
        for hd in range(N_Q_HEADS):
            pair = hd // 2
            q_pair = q_scr[n * BLOCK:(n + 1) * BLOCK, pair * 2 * HEAD_DIM:(pair + 1) * 2 * HEAD_DIM]
            sel = low_half if hd % 2 == 0 else jnp.logical_not(low_half)
            qm = jnp.where(sel, q_pair, jnp.zeros_like(q_pair))
            kvh = hd // (N_Q_HEADS // N_KV_HEADS)
            scores.append(_dot_nt(qm, band(k_scr, n, kvh * 2 * HEAD_DIM)))
        return scores

    def attention_outputs(n, scores):
        outs = []
        for hd, s in enumerate(scores):
            bm = bias_scr[hd]
            if n == 0:
                bm = jnp.where(no_prev_block, NEG_INF, bm)
            s = s + bm
            p = jnp.exp(s - jnp.max(s, axis=-1, keepdims=True)).astype(BF16)
            kvh = hd // (N_Q_HEADS // N_KV_HEADS)
            v_ext = jnp.concatenate(
                [band(vx_scr, n, 2 * kvh * 2 * HEAD_DIM),
                 vx_scr[n * BLOCK:(n + 2) * BLOCK,
                        (2 * kvh + 1) * 2 * HEAD_DIM:(2 * kvh + 2) * 2 * HEAD_DIM]], axis=1)
            o_ext = jnp.dot(p, v_ext, preferred_element_type=F32)
            outs.append(o_ext[:, :2 * HEAD_DIM] / o_ext[:, 2 * HEAD_DIM:])
        for pair in range(N_Q_HEADS // 2):
            ya_scr[n * BLOCK:(n + 1) * BLOCK, pair * 2 * HEAD_DIM:(pair + 1) * 2 * HEAD_DIM] = (
                jnp.where(low_half, outs[2 * pair], outs[2 * pair + 1]).astype(BF16))

    scores = attention_scores(0)
    qk = jnp.dot(h, w_gla_ref[:, :2 * GLA_K_WIDTH], preferred_element_type=F32)
    attention_outputs(0, scores)
    scores = attention_scores(1)
    v_all = jnp.dot(h, w_gla_ref[:, 2 * GLA_K_WIDTH:2 * GLA_K_WIDTH + GLA_V_WIDTH],
                    preferred_element_type=F32).astype(BF16)
    attention_outputs(1, scores)
    q_all = qk[:, :GLA_K_WIDTH]
    k_all = qk[:, GLA_K_WIDTH:]
    q_dec_all = (q_all * jnp.exp(bcum)).astype(BF16)
    k_intra_all = (k_all * jnp.exp(-bcum)).astype(BF16)
    for n in range(2, rows // BLOCK):
        scores = attention_scores(n)
        gate_piece()
        attention_outputs(n, scores)

    chunks = [slice(c * GLA_CHUNK, (c + 1) * GLA_CHUNK) for c in range(rows // GLA_CHUNK)]
    kbd_mask = kbd_scr[...] != 0
    vbd_mask = vbd_scr[...] != 0
    causal = (lax.broadcasted_iota(jnp.int32, (GLA_CHUNK, GLA_K_WIDTH), 0)
              >= lax.broadcasted_iota(jnp.int32, (GLA_CHUNK, GLA_K_WIDTH), 1) % GLA_CHUNK)
    attn, d_state, decay_col = [], [], []
    for cs in chunks:
        k_bd = jnp.where(kbd_mask, jnp.concatenate([k_intra_all[cs, :]] * GLA_HEADS, axis=0),
                         jnp.zeros((), BF16))
        attn.append(jnp.where(causal, _dot_nt(q_dec_all[cs, :], k_bd), 0.0).astype(BF16))
        b_c = bcum[cs, :]
        b_last = b_c[GLA_CHUNK - 1:GLA_CHUNK, :]
        k_state_t = (k_all[cs, :] * jnp.exp(b_last - b_c)).T.astype(BF16)
        d_state.append([jnp.dot(k_state_t[hh * GLA_DK:(hh + 1) * GLA_DK, :],
                                v_all[cs, hh * GLA_DV:(hh + 1) * GLA_DV],
                                preferred_element_type=F32) for hh in range(GLA_HEADS)])
        decay = jnp.exp(b_last)
        decay_col.append(jnp.concatenate(
            [jnp.broadcast_to(decay[:, l0:l0 + 128], (128, 128)).T
             for l0 in range(0, GLA_K_WIDTH, 128)], axis=0))
    gate_piece()

    zero_blk = jnp.zeros((GLA_DK, GLA_DV), BF16)
    state = [st_scr[hh * GLA_DK:(hh + 1) * GLA_DK, :] for hh in range(GLA_HEADS)]
    for c, cs in enumerate(chunks):
        v_bd = jnp.where(vbd_mask, jnp.concatenate([v_all[cs, :]] * GLA_HEADS, axis=0),
                         jnp.zeros((), BF16))
        s_bd = jnp.concatenate(
            [jnp.concatenate([zero_blk] * hh + [state[hh].astype(BF16)]
                             + [zero_blk] * (GLA_HEADS - 1 - hh), axis=1)
             for hh in range(GLA_HEADS)], axis=0)
        og_scr[cs, :] = (jnp.dot(attn[c], v_bd, preferred_element_type=F32)
                         + jnp.dot(q_dec_all[cs, :], s_bd, preferred_element_type=F32))
        state = [state[hh] * decay_col[c][hh * GLA_DK:(hh + 1) * GLA_DK, :] + d_state[c][hh]
                 for hh in range(GLA_HEADS)]
    for hh in range(GLA_HEADS):
        st_scr[hh * GLA_DK:(hh + 1) * GLA_DK, :] = state[hh]

    r_all = jnp.dot(h, w_gla_ref[:, 2 * GLA_K_WIDTH + GLA_V_WIDTH:], preferred_element_type=F32)
    gate_piece()
    assert next(gate_pieces, None) is None
    m_a = jnp.dot(ya_scr[...], w_pa_ref[...], preferred_element_type=F32)
    gla_g = gla_g_ref[...]
    for hh in range(GLA_HEADS):
        lanes = slice(hh * GLA_DV, (hh + 1) * GLA_DV)
        r_h = r_all[:, lanes]
        yb_scr[:, lanes] = (_rmsnorm(og_scr[:, lanes], gla_g) * (r_h * _sigmoid(r_h))).astype(BF16)

    m_b = [jnp.dot(yb_scr[rs, :], w_pb_ref[...], preferred_element_type=F32) for rs in halves]
    mixed = []
    for i, rs in enumerate(halves):
        merged = (sg_scr[rs, :D_MODEL] * m_a[rs, :] + sg_scr[rs, D_MODEL:] * m_b[i]).astype(BF16)
        mixed.append(jnp.dot(merged, w_out_ref[...], preferred_element_type=F32))
    for i, rs in enumerate(halves):
        o_ref[rs, :] = x_ref[rs, :] + _rmsnorm(mixed[i], post_g_ref[...])


def _t5_bucket_table():
    t = np.arange(BLOCK)[:, None]
    j = np.arange(2 * BLOCK)[None, :]
    n = np.clip(t + BLOCK - j, 0, REL_MAX_DIST - 1)
    max_exact = REL_BUCKETS // 2
    nf = np.maximum(n, 1).astype(np.float32)
    large = max_exact + (np.log(nf / np.float32(max_exact))
                         / np.float32(math.log(REL_MAX_DIST / max_exact))
                         * np.float32(REL_BUCKETS - max_exact)).astype(np.int32)
    large = np.minimum(large, REL_BUCKETS - 1)
    return np.where(n < max_exact, n, large).astype(np.int32)


def _mixer(x2d, seq_len, rel_bias, pre_g, w_in, w_alpha, b_alpha, sinks, gla_g,
           w_proj_a, w_proj_b, w_out, post_g):
    n_rows = x2d.shape[0]
    offs = np.concatenate([[0], np.cumsum(IN_SPLITS)])
    (o_q, o_k, o_v, o_qb, o_kb, _, _, o_alr, o_ga, o_end) = (int(offs[i]) for i in
                                                            (0, 1, 2, 3, 4, 5, 6, 7, 8, 10))

    def dup_heads(o):
        return [w_in[:, o + hd * HEAD_DIM:o + (hd + 1) * HEAD_DIM]
                for hd in range(N_KV_HEADS) for _ in range(2)]

    w_att = jnp.concatenate([w_in[:, o_q:o_k] * (HEAD_DIM ** -0.5)] + dup_heads(o_k) + dup_heads(o_v),
                            axis=1).astype(BF16)
    w_gla = jnp.concatenate([w_in[:, o_qb:o_kb] * (GLA_DK ** -0.5), w_in[:, o_kb:o_alr]],
                            axis=1).astype(BF16)
    w_alr_p = jnp.pad(w_in[:, o_alr:o_ga], ((0, 0), (0, ALR_PAD - GLA_GATE_RANK))).astype(BF16)
    w_alpha_p = jnp.pad(w_alpha, ((0, ALR_PAD - GLA_GATE_RANK), (0, 0))).astype(BF16)
    w_gate = w_in[:, o_ga:o_end].astype(BF16)
    bucket = jnp.asarray(_t5_bucket_table())

    steps = seq_len // MIX_ROWS
    smem = pl.BlockSpec(memory_space=pltpu.SMEM)
    row_spec = pl.BlockSpec((MIX_ROWS, D_MODEL), lambda b, j: (b * steps + j, 0))
    return pl.pallas_call(
        _mixer_kernel,
        out_shape=jax.ShapeDtypeStruct((n_rows, D_MODEL), F32),
        grid=(n_rows // seq_len, steps),
        in_specs=[
            smem, smem, row_spec,
            _const_spec((1, D_MODEL)),
            _const_spec(w_att.shape), _const_spec(w_gla.shape), _const_spec(w_alr_p.shape),
            _const_spec(w_alpha_p.shape), _const_spec((1, GLA_K_WIDTH)),
            _const_spec(w_gate.shape), _const_spec((1, GLA_DV)),
            _const_spec((ATT_WIDTH, D_MODEL)), _const_spec((GLA_V_WIDTH, D_MODEL)),
            _const_spec((D_MODEL, D_MODEL)), _const_spec((1, D_MODEL)),
            _const_spec(bucket.shape),
        ],
        out_specs=row_spec,
        scratch_shapes=[
            pltpu.VMEM((MIX_ROWS + BLOCK, KV2_WIDTH), BF16),
            pltpu.VMEM((MIX_ROWS + BLOCK, 2 * KV2_WIDTH), BF16),
            pltpu.VMEM((GLA_K_WIDTH, GLA_DV), F32),
            pltpu.VMEM((N_Q_HEADS, BLOCK, 2 * BLOCK), F32),
            pltpu.VMEM((TRIL_ROWS, TRIL_ROWS), BF16),
            pltpu.VMEM((GLA_HEADS * GLA_CHUNK, GLA_K_WIDTH), BF16),
            pltpu.VMEM((GLA_HEADS * GLA_CHUNK, GLA_V_WIDTH), BF16),
            pltpu.VMEM((MIX_ROWS, D_MODEL), BF16),
            pltpu.VMEM((MIX_ROWS, ATT_WIDTH), BF16),
            pltpu.VMEM((MIX_ROWS, ATT_WIDTH), BF16),
            pltpu.VMEM((MIX_ROWS, GLA_V_WIDTH), BF16),
            pltpu.VMEM((MIX_ROWS, GLA_V_WIDTH), F32),
            pltpu.VMEM((MIX_ROWS, 2 * D_MODEL), F32),
        ],
        compiler_params=pltpu.CompilerParams(
            dimension_semantics=("arbitrary", "arbitrary"),
            vmem_limit_bytes=VMEM_LIMIT_BYTES),
        name="mixer",
    )(rel_bias, sinks, x2d, pre_g.reshape(1, D_MODEL), w_att, w_gla, w_alr_p, w_alpha_p,
      b_alpha.reshape(1, GLA_K_WIDTH), w_gate, gla_g.reshape(1, GLA_DV),
      w_proj_a.astype(BF16), w_proj_b.astype(BF16), w_out.astype(BF16),
      post_g.reshape(1, D_MODEL), bucket)


def kernel(x, rel_bias, ffn1_pre_g, ffn1_w_gate, ffn1_w_up, ffn1_w_down, ffn1_post_g, mix_pre_g, w_in, w_alpha, b_alpha, attn_sinks, gla_norm_g, w_proj_a, w_proj_b, w_out, mix_post_g, ffn2_pre_g, ffn2_w_gate, ffn2_w_up, ffn2_w_down, ffn2_post_g):
    b, s, d = x.shape
    x2d = x.reshape(b * s, d)
    x2d = _ffn(x2d, ffn1_pre_g[0], ffn1_w_gate[0], ffn1_w_up[0], ffn1_w_down[0], ffn1_post_g[0])
    x2d = _mixer(x2d, s, rel_bias, mix_pre_g[0], w_in[0], w_alpha[0], b_alpha[0], attn_sinks[0],
                 gla_norm_g[0], w_proj_a[0], w_proj_b[0], w_out[0], mix_post_g[0])
    x2d = _ffn(x2d, ffn2_pre_g[0], ffn2_w_gate[0], ffn2_w_up[0], ffn2_w_down[0], ffn2_post_g[0])
    return x2d.reshape(b, s, d)
```

```python
import functools
import math

import numpy as np
import jax
import jax.numpy as jnp
from jax import lax
from jax.experimental import pallas as pl
from jax.experimental.pallas import tpu as pltpu

D_MODEL = 1024
D_FF = 2816
EPS = 1e-6

N_Q_HEADS = 8
N_KV_HEADS = 2
HEAD_DIM = 64
WINDOW = 128
BLOCK = 128
ATT_WIDTH = N_Q_HEADS * HEAD_DIM
KV_WIDTH = N_KV_HEADS * HEAD_DIM
KV2_WIDTH = 2 * KV_WIDTH
BF16_SUBLANES = 16
REL_BUCKETS = 32
REL_MAX_DIST = 128
NEG_INF = -1e30
GLA_HEADS = 4
GLA_DK = 64
GLA_DV = 128
GLA_K_WIDTH = GLA_HEADS * GLA_DK
GLA_V_WIDTH = GLA_HEADS * GLA_DV
GLA_GATE_RANK = 16
GLA_TAU = 16.0
GLA_CHUNK = 64
ALR_PAD = 128
IN_SPLITS = (ATT_WIDTH, KV_WIDTH, KV_WIDTH, GLA_K_WIDTH, GLA_K_WIDTH, GLA_V_WIDTH,
             GLA_V_WIDTH, GLA_GATE_RANK, D_MODEL, D_MODEL)
MIX_ROWS = 512
TRIL_ROWS = 256
GATE_PIECE_COLS = 512
N_GATE_PIECES = 2 * D_MODEL // GATE_PIECE_COLS

FF_CHUNK = 256
N_FF_CHUNKS = D_FF // FF_CHUNK
FFN_SUB_ROWS = 512
FFN_ROWS = 2 * FFN_SUB_ROWS

VMEM_LIMIT_BYTES = 56 * 1024 * 1024

F32 = jnp.float32
BF16 = jnp.bfloat16


def _rmsnorm(x, g):
    ms = jnp.mean(x * x, axis=-1, keepdims=True)
    return x * lax.rsqrt(ms + EPS) * g


def _sigmoid(x):
    return 1.0 / (1.0 + jnp.exp(-x))


def _const_spec(shape):
    return pl.BlockSpec(shape, lambda *_: (0,) * len(shape),
                        pipeline_mode=pl.Buffered(1))


def _ffn_kernel(x_ref, pre_g_ref, wg_ref, wu_ref, wd_ref, post_g_ref, o_ref):
    starts = range(0, x_ref.shape[0], FFN_SUB_ROWS)
    xn = [_rmsnorm(x_ref[r0:r0 + FFN_SUB_ROWS, :], pre_g_ref[...]).astype(BF16) for r0 in starts]
    acc = [jnp.zeros((FFN_SUB_ROWS, D_MODEL), F32) for _ in starts]
    for c in range(N_FF_CHUNKS):
        cols = slice(c * FF_CHUNK, (c + 1) * FF_CHUNK)
        for t, r0 in enumerate(starts):
            g = jnp.dot(xn[t], wg_ref[:, cols], preferred_element_type=F32)
            u = jnp.dot(xn[t], wu_ref[:, cols], preferred_element_type=F32)
            h = (g * _sigmoid(g) * u).astype(BF16)
            acc[t] = acc[t] + jnp.dot(h, wd_ref[cols, :], preferred_element_type=F32)
            if c == N_FF_CHUNKS - 1:
                o_ref[r0:r0 + FFN_SUB_ROWS, :] = (
                    x_ref[r0:r0 + FFN_SUB_ROWS, :] + 0.5 * _rmsnorm(acc[t], post_g_ref[...]))


def _ffn(x2d, pre_g, w_gate, w_up, w_down, post_g):
    n_rows = x2d.shape[0]
    return pl.pallas_call(
        _ffn_kernel,
        out_shape=jax.ShapeDtypeStruct((n_rows, D_MODEL), F32),
        grid=(n_rows // FFN_ROWS,),
        in_specs=[
            pl.BlockSpec((FFN_ROWS, D_MODEL), lambda i: (i, 0)),
            _const_spec((1, D_MODEL)),
            _const_spec((D_MODEL, D_FF)),
            _const_spec((D_MODEL, D_FF)),
            _const_spec((D_FF, D_MODEL)),
            _const_spec((1, D_MODEL)),
        ],
        out_specs=pl.BlockSpec((FFN_ROWS, D_MODEL), lambda i: (i, 0)),
        compiler_params=pltpu.CompilerParams(
            dimension_semantics=("arbitrary",),
            vmem_limit_bytes=VMEM_LIMIT_BYTES),
        name="ffn",
    )(x2d, pre_g.reshape(1, D_MODEL), w_gate.astype(BF16), w_up.astype(BF16),
      w_down.astype(BF16), post_g.reshape(1, D_MODEL))


def _dot_nt(a, b):
    return lax.dot_general(a, b, (((1,), (1,)), ((), ())), preferred_element_type=F32)


def _log_sigmoid(z):
    return jnp.minimum(z, 0.0) - jnp.log1p(jnp.exp(-jnp.abs(z)))


def _block_mask(shape, row_block, col_block):
    row = lax.broadcasted_iota(jnp.int32, shape, 0) // row_block
    col = lax.broadcasted_iota(jnp.int32, shape, 1) // col_block
    return row == col


def _mixer_kernel(rel_ref, sinks_ref, x_ref, pre_g_ref, w_att_ref, w_gla_ref, w_alr_ref,
                  w_alpha_ref, b_alpha_ref, w_gate_ref, gla_g_ref, w_pa_ref, w_pb_ref,
                  w_out_ref, post_g_ref, bucket_ref, o_ref,
                  k_scr, vx_scr, st_scr, bias_scr, tril_scr, kbd_scr, vbd_scr,
                  h_scr, q_scr, ya_scr, yb_scr, og_scr, sg_scr):
    rows = x_ref.shape[0]
    seq_step = pl.program_id(1)
    first_of_seq = seq_step == 0

    @pl.when(jnp.logical_and(pl.program_id(0) == 0, first_of_seq))
    def _():
        bucket = bucket_ref[...]
        shape = (BLOCK, 2 * BLOCK)
        dist = (lax.broadcasted_iota(jnp.int32, shape, 0) + BLOCK
                - lax.broadcasted_iota(jnp.int32, shape, 1))
        valid = jnp.logical_and(dist >= 0, dist < WINDOW)
        for hd in range(N_Q_HEADS):
            def body(bk, acc, hd=hd):
                return jnp.where(bucket == bk, rel_ref[bk, hd], acc)
            bias = lax.fori_loop(0, REL_BUCKETS, body, jnp.zeros(shape, F32))
            bias_scr[hd] = jnp.where(lax.broadcasted_iota(jnp.int32, shape, 1) == 0,
                                     sinks_ref[hd], jnp.where(valid, bias, NEG_INF))
        ones = jnp.ones((vx_scr.shape[0], 2 * HEAD_DIM), BF16)
        for kvh in range(N_KV_HEADS):
            vx_scr[:, (2 * kvh + 1) * 2 * HEAD_DIM:(2 * kvh + 2) * 2 * HEAD_DIM] = ones
        tshape = (TRIL_ROWS, TRIL_ROWS)
        trow = lax.broadcasted_iota(jnp.int32, tshape, 0)
        tcol = lax.broadcasted_iota(jnp.int32, tshape, 1)
        tril_scr[...] = jnp.logical_and(trow // GLA_CHUNK == tcol // GLA_CHUNK,
                                        tcol <= trow).astype(F32).astype(BF16)
        kbd_scr[...] = _block_mask(kbd_scr.shape, GLA_CHUNK, GLA_DK).astype(F32).astype(BF16)
        vbd_scr[...] = _block_mask(vbd_scr.shape, GLA_CHUNK, GLA_DV).astype(F32).astype(BF16)

    @pl.when(first_of_seq)
    def _():
        k_scr[0:BLOCK, :] = jnp.zeros((BLOCK, k_scr.shape[1]), BF16)
        for kvh in range(N_KV_HEADS):
            vx_scr[0:BLOCK, 2 * kvh * 2 * HEAD_DIM:(2 * kvh + 1) * 2 * HEAD_DIM] = (
                jnp.zeros((BLOCK, 2 * HEAD_DIM), BF16))
        st_scr[...] = jnp.zeros(st_scr.shape, F32)

    @pl.when(jnp.logical_not(first_of_seq))
    def _():
        k_scr[0:BLOCK, :] = k_scr[rows:rows + BLOCK, :]
        vx_scr[0:BLOCK, :] = vx_scr[rows:rows + BLOCK, :]

    half_rows = rows // 2
    halves = [slice(i * half_rows, (i + 1) * half_rows) for i in range(2)]
    for rs in halves:
        h_scr[rs, :] = _rmsnorm(x_ref[rs, :], pre_g_ref[...]).astype(BF16)
    h = h_scr[...]

    def attention_projection(rs):
        att = jnp.dot(h_scr[rs, :], w_att_ref[...], preferred_element_type=F32)
        q_scr[rs, :] = (att[:, :ATT_WIDTH] * (HEAD_DIM ** -0.5)).astype(BF16)
        new_rows = slice(BLOCK + rs.start, BLOCK + rs.stop)
        k = att[:, ATT_WIDTH:ATT_WIDTH + KV_WIDTH]
        v = att[:, ATT_WIDTH + KV_WIDTH:]
        low = lax.broadcasted_iota(jnp.int32, k.shape, 1) < HEAD_DIM
        k_swap = pltpu.roll(k, HEAD_DIM, 1)
        v_swap = pltpu.roll(v, HEAD_DIM, 1)
        for kvh in range(N_KV_HEADS):
            keep = low if kvh == 0 else jnp.logical_not(low)
            k_scr[new_rows, kvh * 2 * HEAD_DIM:(kvh + 1) * 2 * HEAD_DIM] = (
                jnp.where(keep, k, k_swap).astype(BF16))
            vx_scr[new_rows, 2 * kvh * 2 * HEAD_DIM:(2 * kvh + 1) * 2 * HEAD_DIM] = (
                jnp.where(keep, v, v_swap).astype(BF16))

    gate_pieces = iter(range(N_GATE_PIECES))

    def gate_piece():
        i = next(gate_pieces)
        gc = slice(i * GATE_PIECE_COLS, (i + 1) * GATE_PIECE_COLS)
        sg_scr[:, gc] = _sigmoid(jnp.dot(h, w_gate_ref[:, gc], preferred_element_type=F32))

    a_lr = jnp.dot(h, w_alr_ref[...], preferred_element_type=F32).astype(BF16)
    attention_projection(halves[0])
    z = jnp.dot(a_lr, w_alpha_ref[...], preferred_element_type=F32) + b_alpha_ref[...]
    attention_projection(halves[1])
    log_a = _log_sigmoid(z) / GLA_TAU
    la_hi = log_a.astype(BF16)
    la_lo = (log_a - la_hi.astype(F32)).astype(BF16)
    tril = tril_scr[...]
    bcum = jnp.concatenate(
        [jnp.dot(tril, la_hi[g0:g0 + TRIL_ROWS, :], preferred_element_type=F32)
         + jnp.dot(tril, la_lo[g0:g0 + TRIL_ROWS, :], preferred_element_type=F32)
         for g0 in range(0, rows, TRIL_ROWS)], axis=0)

    pair_shape = (BLOCK, 2 * HEAD_DIM)
    low_half = lax.broadcasted_iota(jnp.int32, pair_shape, 1) < HEAD_DIM
    band_col = lax.broadcasted_iota(jnp.int32, (BLOCK, 2 * BLOCK), 1)
    no_prev_block = jnp.logical_and(first_of_seq,
                                    jnp.logical_and(band_col > 0, band_col < BLOCK))
    top = BF16_SUBLANES
    row0 = lax.broadcasted_iota(jnp.int32, (top, 2 * HEAD_DIM), 0) == 0

    def band(ref, n, c0):
        r0 = n * BLOCK
        head = jnp.where(row0, jnp.zeros((), BF16), ref[r0:r0 + top, c0:c0 + 2 * HEAD_DIM])
        return jnp.concatenate([head, ref[r0 + top:r0 + 2 * BLOCK, c0:c0 + 2 * HEAD_DIM]], axis=0)

    def attention_scores(n):
        scores = []
        for hd in range(N_Q_HEADS):
            pair = hd // 2
            q_pair = q_scr[n * BLOCK:(n + 1) * BLOCK, pair * 2 * HEAD_DIM:(pair + 1) * 2 * HEAD_DIM]
            sel = low_half if hd % 2 == 0 else jnp.logical_not(low_half)
            qm = jnp.where(sel, q_pair, jnp.zeros_like(q_pair))
            kvh = hd // (N_Q_HEADS // N_KV_HEADS)
            scores.append(_dot_nt(qm, band(k_scr, n, kvh * 2 * HEAD_DIM)))
        return scores

    def attention_outputs(n, scores):
        outs = []
        for hd, s in enumerate(scores):
            bm = bias_scr[hd]
            if n == 0:
                bm = jnp.where(no_prev_block, NEG_INF, bm)
            s = s + bm
            p = jnp.exp(s - jnp.max(s, axis=-1, keepdims=True)).astype(BF16)
            kvh = hd // (N_Q_HEADS // N_KV_HEADS)
            v_ext = jnp.concatenate(
                [band(vx_scr, n, 2 * kvh * 2 * HEAD_DIM),
                 vx_scr[n * BLOCK:(n + 2) * BLOCK,
                        (2 * kvh + 1) * 2 * HEAD_DIM:(2 * kvh + 2) * 2 * HEAD_DIM]], axis=1)
            o_ext = jnp.dot(p, v_ext, preferred_element_type=F32)
            outs.append(o_ext[:, :2 * HEAD_DIM] / o_ext[:, 2 * HEAD_DIM:])
        for pair in range(N_Q_HEADS // 2):
            ya_scr[n * BLOCK:(n + 1) * BLOCK, pair * 2 * HEAD_DIM:(pair + 1) * 2 * HEAD_DIM] = (
                jnp.where(low_half, outs[2 * pair], outs[2 * pair + 1]).astype(BF16))

    scores = attention_scores(0)
    qk = jnp.dot(h, w_gla_ref[:, :2 * GLA_K_WIDTH], preferred_element_type=F32)
    attention_outputs(0, scores)
    scores = attention_scores(1)
    v_all = jnp.dot(h, w_gla_ref[:, 2 * GLA_K_WIDTH:2 * GLA_K_WIDTH + GLA_V_WIDTH],
                    preferred_element_type=F32).astype(BF16)
    attention_outputs(1, scores)
    q_all = qk[:, :GLA_K_WIDTH] * (GLA_DK ** -0.5)
    k_all = qk[:, GLA_K_WIDTH:]
    q_dec_all = (q_all * jnp.exp(bcum)).astype(BF16)
    k_intra_all = (k_all * jnp.exp(-bcum)).astype(BF16)
    for n in range(2, rows // BLOCK):
        scores = attention_scores(n)
        gate_piece()
        attention_outputs(n, scores)

    chunks = [slice(c * GLA_CHUNK, (c + 1) * GLA_CHUNK) for c in range(rows // GLA_CHUNK)]
    kbd_mask = kbd_scr[...] != 0
    vbd_mask = vbd_scr[...] != 0
    causal = (lax.broadcasted_iota(jnp.int32, (GLA_CHUNK, GLA_K_WIDTH), 0)
              >= lax.broadcasted_iota(jnp.int32, (GLA_CHUNK, GLA_K_WIDTH), 1) % GLA_CHUNK)
    attn, d_state, decay_col = [], [], []
    for cs in chunks:
        k_bd = jnp.where(kbd_mask, jnp.concatenate([k_intra_all[cs, :]] * GLA_HEADS, axis=0),
                         jnp.zeros((), BF16))
        attn.append(jnp.where(causal, _dot_nt(q_dec_all[cs, :], k_bd), 0.0).astype(BF16))
        b_c = bcum[cs, :]
        b_last = b_c[GLA_CHUNK - 1:GLA_CHUNK, :]
        k_state_t = (k_all[cs, :] * jnp.exp(b_last - b_c)).T.astype(BF16)
        d_state.append([jnp.dot(k_state_t[hh * GLA_DK:(hh + 1) * GLA_DK, :],
                                v_all[cs, hh * GLA_DV:(hh + 1) * GLA_DV],
                                preferred_element_type=F32) for hh in range(GLA_HEADS)])
        decay = jnp.exp(b_last)
        decay_col.append(jnp.concatenate(
            [jnp.broadcast_to(decay[:, l0:l0 + 128], (128, 128)).T
             for l0 in range(0, GLA_K_WIDTH, 128)], axis=0))
    gate_piece()

    zero_blk = jnp.zeros((GLA_DK, GLA_DV), BF16)
    state = [st_scr[hh * GLA_DK:(hh + 1) * GLA_DK, :] for hh in range(GLA_HEADS)]
    for c, cs in enumerate(chunks):
        v_bd = jnp.where(vbd_mask, jnp.concatenate([v_all[cs, :]] * GLA_HEADS, axis=0),
                         jnp.zeros((), BF16))
        s_bd = jnp.concatenate(
            [jnp.concatenate([zero_blk] * hh + [state[hh].astype(BF16)]
                             + [zero_blk] * (GLA_HEADS - 1 - hh), axis=1)
             for hh in range(GLA_HEADS)], axis=0)
        og_scr[cs, :] = (jnp.dot(attn[c], v_bd, preferred_element_type=F32)
                         + jnp.dot(q_dec_all[cs, :], s_bd, preferred_element_type=F32))
        state = [state[hh] * decay_col[c][hh * GLA_DK:(hh + 1) * GLA_DK, :] + d_state[c][hh]
                 for hh in range(GLA_HEADS)]
    for hh in range(GLA_HEADS):
        st_scr[hh * GLA_DK:(hh + 1) * GLA_DK, :] = state[hh]

    r_all = jnp.dot(h, w_gla_ref[:, 2 * GLA_K_WIDTH + GLA_V_WIDTH:], preferred_element_type=F32)
    gate_piece()
    assert next(gate_pieces, None) is None
    m_a = jnp.dot(ya_scr[...], w_pa_ref[...], preferred_element_type=F32)
    gla_g = gla_g_ref[...]
    for hh in range(GLA_HEADS):
        lanes = slice(hh * GLA_DV, (hh + 1) * GLA_DV)
        r_h = r_all[:, lanes]
        yb_scr[:, lanes] = (_rmsnorm(og_scr[:, lanes], gla_g) * (r_h * _sigmoid(r_h))).astype(BF16)

    m_b = [jnp.dot(yb_scr[rs, :], w_pb_ref[...], preferred_element_type=F32) for rs in halves]
    mixed = []
    for i, rs in enumerate(halves):
        merged = (sg_scr[rs, :D_MODEL] * m_a[rs, :] + sg_scr[rs, D_MODEL:] * m_b[i]).astype(BF16)
        mixed.append(jnp.dot(merged, w_out_ref[...], preferred_element_type=F32))
    for i, rs in enumerate(halves):
        o_ref[rs, :] = x_ref[rs, :] + _rmsnorm(mixed[i], post_g_ref[...])


def _t5_bucket_table():
    t = np.arange(BLOCK)[:, None]
    j = np.arange(2 * BLOCK)[None, :]
    n = np.clip(t + BLOCK - j, 0, REL_MAX_DIST - 1)
    max_exact = REL_BUCKETS // 2
    nf = np.maximum(n, 1).astype(np.float32)
    large = max_exact + (np.log(nf / np.float32(max_exact))
                         / np.float32(math.log(REL_MAX_DIST / max_exact))
                         * np.float32(REL_BUCKETS - max_exact)).astype(np.int32)
    large = np.minimum(large, REL_BUCKETS - 1)
    return np.where(n < max_exact, n, large).astype(np.int32)


def _mixer(x2d, seq_len, rel_bias, pre_g, w_in, w_alpha, b_alpha, sinks, gla_g,
           w_proj_a, w_proj_b, w_out, post_g):
    n_rows = x2d.shape[0]
    offs = np.concatenate([[0], np.cumsum(IN_SPLITS)])
    o_q, o_qb, o_alr, o_ga, o_end = (int(offs[i]) for i in (0, 3, 7, 8, 10))
    w_att = w_in[:, o_q:o_qb].astype(BF16)
    w_gla = w_in[:, o_qb:o_alr].astype(BF16)
    w_alr_p = jnp.pad(w_in[:, o_alr:o_ga], ((0, 0), (0, ALR_PAD - GLA_GATE_RANK))).astype(BF16)
    w_alpha_p = jnp.pad(w_alpha, ((0, ALR_PAD - GLA_GATE_RANK), (0, 0))).astype(BF16)
    w_gate = w_in[:, o_ga:o_end].astype(BF16)
    bucket = jnp.asarray(_t5_bucket_table())

    steps = seq_len // MIX_ROWS
    smem = pl.BlockSpec(memory_space=pltpu.SMEM)
    row_spec = pl.BlockSpec((MIX_ROWS, D_MODEL), lambda b, j: (b * steps + j, 0))
    return pl.pallas_call(
        _mixer_kernel,
        out_shape=jax.ShapeDtypeStruct((n_rows, D_MODEL), F32),
        grid=(n_rows // seq_len, steps),
        in_specs=[
            smem, smem, row_spec,
            _const_spec((1, D_MODEL)),
            _const_spec(w_att.shape), _const_spec(w_gla.shape), _const_spec(w_alr_p.shape),
            _const_spec(w_alpha_p.shape), _const_spec((1, GLA_K_WIDTH)),
            _const_spec(w_gate.shape), _const_spec((1, GLA_DV)),
            _const_spec((ATT_WIDTH, D_MODEL)), _const_spec((GLA_V_WIDTH, D_MODEL)),
            _const_spec((D_MODEL, D_MODEL)), _const_spec((1, D_MODEL)),
            _const_spec(bucket.shape),
        ],
        out_specs=row_spec,
        scratch_shapes=[
            pltpu.VMEM((MIX_ROWS + BLOCK, KV2_WIDTH), BF16),
            pltpu.VMEM((MIX_ROWS + BLOCK, 2 * KV2_WIDTH), BF16),
            pltpu.VMEM((GLA_K_WIDTH, GLA_DV), F32),
            pltpu.VMEM((N_Q_HEADS, BLOCK, 2 * BLOCK), F32),
            pltpu.VMEM((TRIL_ROWS, TRIL_ROWS), BF16),
            pltpu.VMEM((GLA_HEADS * GLA_CHUNK, GLA_K_WIDTH), BF16),
            pltpu.VMEM((GLA_HEADS * GLA_CHUNK, GLA_V_WIDTH), BF16),
            pltpu.VMEM((MIX_ROWS, D_MODEL), BF16),
            pltpu.VMEM((MIX_ROWS, ATT_WIDTH), BF16),
            pltpu.VMEM((MIX_ROWS, ATT_WIDTH), BF16),
            pltpu.VMEM((MIX_ROWS, GLA_V_WIDTH), BF16),
            pltpu.VMEM((MIX_ROWS, GLA_V_WIDTH), F32),
            pltpu.VMEM((MIX_ROWS, 2 * D_MODEL), F32),
        ],
        compiler_params=pltpu.CompilerParams(
            dimension_semantics=("arbitrary", "arbitrary"),
            vmem_limit_bytes=VMEM_LIMIT_BYTES),
        name="mixer",
    )(rel_bias, sinks, x2d, pre_g.reshape(1, D_MODEL), w_att, w_gla, w_alr_p, w_alpha_p,
      b_alpha.reshape(1, GLA_K_WIDTH), w_gate, gla_g.reshape(1, GLA_DV),
      w_proj_a.astype(BF16), w_proj_b.astype(BF16), w_out.astype(BF16),
      post_g.reshape(1, D_MODEL), bucket)


def kernel(x, rel_bias, ffn1_pre_g, ffn1_w_gate, ffn1_w_up, ffn1_w_down, ffn1_post_g, mix_pre_g, w_in, w_alpha, b_alpha, attn_sinks, gla_norm_g, w_proj_a, w_proj_b, w_out, mix_post_g, ffn2_pre_g, ffn2_w_gate, ffn2_w_up, ffn2_w_down, ffn2_post_g):
    b, s, d = x.shape
    x2d = x.reshape(b * s, d)
    x2d = _ffn(x2d, ffn1_pre_g[0], ffn1_w_gate[0], ffn1_w_up[0], ffn1_w_down[0], ffn1_post_g[0])
    x2d = _mixer(x2d, s, rel_bias, mix_pre_g[0], w_in[0], w_alpha[0], b_alpha[0], attn_sinks[0],
                 gla_norm_g[0], w_proj_a[0], w_proj_b[0], w_out[0], mix_post_g[0])
    x2d = _ffn(x2d, ffn2_pre_g[0], ffn2_w_gate[0], ffn2_w_up[0], ffn2_w_down[0], ffn2_post_g[0])
    return x2d.reshape(b, s, d)
```

```python
import functools
import math

import numpy as np
import jax
import jax.numpy as jnp
from jax import lax
from jax.experimental import pallas as pl
from jax.experimental.pallas import tpu as pltpu

D_MODEL = 1024
D_FF = 2816
EPS = 1e-6

N_Q_HEADS = 8
N_KV_HEADS = 2
HEAD_DIM = 64
WINDOW = 128
BLOCK = 128
ATT_WIDTH = N_Q_HEADS * HEAD_DIM
KV_WIDTH = N_KV_HEADS * HEAD_DIM
KV2_WIDTH = 2 * KV_WIDTH
BF16_SUBLANES = 16
REL_BUCKETS = 32
REL_MAX_DIST = 128
NEG_INF = -1e30
GLA_HEADS = 4
GLA_DK = 64
GLA_DV = 128
GLA_K_WIDTH = GLA_HEADS * GLA_DK
GLA_V_WIDTH = GLA_HEADS * GLA_DV
GLA_GATE_RANK = 16
GLA_TAU = 16.0
GLA_CHUNK = 64
ALR_PAD = 128
IN_SPLITS = (ATT_WIDTH, KV_WIDTH, KV_WIDTH, GLA_K_WIDTH, GLA_K_WIDTH, GLA_V_WIDTH,
             GLA_V_WIDTH, GLA_GATE_RANK, D_MODEL, D_MODEL)
MIX_ROWS = 512
TRIL_ROWS = 256
GATE_PIECE_COLS = 512
N_GATE_PIECES = 2 * D_MODEL // GATE_PIECE_COLS

FF_CHUNK = 256
N_FF_CHUNKS = D_FF // FF_CHUNK
FFN_SUB_ROWS = 512
FFN_ROWS = 2 * FFN_SUB_ROWS

VMEM_LIMIT_BYTES = 56 * 1024 * 1024

F32 = jnp.float32
BF16 = jnp.bfloat16


def _rmsnorm(x, g):
    ms = jnp.mean(x * x, axis=-1, keepdims=True)
    return x * lax.rsqrt(ms + EPS) * g


def _sigmoid(x):
    return 1.0 / (1.0 + jnp.exp(-x))


def _const_spec(shape):
    return pl.BlockSpec(shape, lambda *_: (0,) * len(shape),
                        pipeline_mode=pl.Buffered(1))


def _ffn_kernel(x_ref, pre_g_ref, wg_ref, wu_ref, wd_ref, post_g_ref, o_ref):
    starts = range(0, x_ref.shape[0], FFN_SUB_ROWS)
    xn = [_rmsnorm(x_ref[r0:r0 + FFN_SUB_ROWS, :], pre_g_ref[...]).astype(BF16) for r0 in starts]
    acc = [jnp.zeros((FFN_SUB_ROWS, D_MODEL), F32) for _ in starts]
    for c in range(N_FF_CHUNKS):
        cols = slice(c * FF_CHUNK, (c + 1) * FF_CHUNK)
        for t, r0 in enumerate(starts):
            g = jnp.dot(xn[t], wg_ref[:, cols], preferred_element_type=F32)
            u = jnp.dot(xn[t], wu_ref[:, cols], preferred_element_type=F32)
            h = (g * _sigmoid(g) * u).astype(BF16)
            acc[t] = acc[t] + jnp.dot(h, wd_ref[cols, :], preferred_element_type=F32)
            if c == N_FF_CHUNKS - 1:
                o_ref[r0:r0 + FFN_SUB_ROWS, :] = (
                    x_ref[r0:r0 + FFN_SUB_ROWS, :] + 0.5 * _rmsnorm(acc[t], post_g_ref[...]))


def _ffn(x2d, pre_g, w_gate, w_up, w_down, post_g):
    n_rows = x2d.shape[0]
    return pl.pallas_call(
        _ffn_kernel,
        out_shape=jax.ShapeDtypeStruct((n_rows, D_MODEL), F32),
        grid=(n_rows // FFN_ROWS,),
        in_specs=[
            pl.BlockSpec((FFN_ROWS, D_MODEL), lambda i: (i, 0)),
            _const_spec((1, D_MODEL)),
            _const_spec((D_MODEL, D_FF)),
            _const_spec((D_MODEL, D_FF)),
            _const_spec((D_FF, D_MODEL)),
            _const_spec((1, D_MODEL)),
        ],
        out_specs=pl.BlockSpec((FFN_ROWS, D_MODEL), lambda i: (i, 0)),
        compiler_params=pltpu.CompilerParams(
            dimension_semantics=("arbitrary",),
            vmem_limit_bytes=VMEM_LIMIT_BYTES),
        name="ffn",
    )(x2d, pre_g.reshape(1, D_MODEL), w_gate.astype(BF16), w_up.astype(BF16),
      w_down.astype(BF16), post_g.reshape(1, D_MODEL))


def _dot_nt(a, b):
    return lax.dot_general(a, b, (((1,), (1,)), ((), ())), preferred_element_type=F32)


def _log_sigmoid(z):
    return jnp.minimum(z, 0.0) - jnp.log1p(jnp.exp(-jnp.abs(z)))


def _block_mask(shape, row_block, col_block):
    row = lax.broadcasted_iota(jnp.int32, shape, 0) // row_block
    col = lax.broadcasted_iota(jnp.int32, shape, 1) // col_block
    return row == col


def _mixer_kernel(rel_ref, sinks_ref, x_ref, pre_g_ref, w_att_ref, w_gla_ref, w_alr_ref,
                  w_alpha_ref, b_alpha_ref, w_gate_ref, gla_g_ref, w_pa_ref, w_pb_ref,
                  w_out_ref, post_g_ref, bucket_ref, o_ref,
                  k_scr, vx_scr, st_scr, bias_scr, tril_scr, kbd_scr, vbd_scr,
                  h_scr, q_scr, ya_scr, yb_scr, og_scr, sg_scr):
    rows = x_ref.shape[0]
    seq_step = pl.program_id(1)
    first_of_seq = seq_step == 0

    @pl.when(jnp.logical_and(pl.program_id(0) == 0, first_of_seq))
    def _():
        bucket = bucket_ref[...]
        shape = (BLOCK, 2 * BLOCK)
        dist = (lax.broadcasted_iota(jnp.int32, shape, 0) + BLOCK
                - lax.broadcasted_iota(jnp.int32, shape, 1))
        valid = jnp.logical_and(dist >= 0, dist < WINDOW)
        for hd in range(N_Q_HEADS):
            def body(bk, acc, hd=hd):
                return jnp.where(bucket == bk, rel_ref[bk, hd], acc)
            bias = lax.fori_loop(0, REL_BUCKETS, body, jnp.zeros(shape, F32))
            bias_scr[hd] = jnp.where(lax.broadcasted_iota(jnp.int32, shape, 1) == 0,
                                     sinks_ref[hd], jnp.where(valid, bias, NEG_INF))
        ones = jnp.ones((vx_scr.shape[0], 2 * HEAD_DIM), BF16)
        for kvh in range(N_KV_HEADS):
            vx_scr[:, (2 * kvh + 1) * 2 * HEAD_DIM:(2 * kvh + 2) * 2 * HEAD_DIM] = ones
        tshape = (TRIL_ROWS, TRIL_ROWS)
        trow = lax.broadcasted_iota(jnp.int32, tshape, 0)
        tcol = lax.broadcasted_iota(jnp.int32, tshape, 1)
        tril_scr[...] = jnp.logical_and(trow // GLA_CHUNK == tcol // GLA_CHUNK,
                                        tcol <= trow).astype(F32).astype(BF16)
        kbd_scr[...] = _block_mask(kbd_scr.shape, GLA_CHUNK, GLA_DK).astype(F32).astype(BF16)
        vbd_scr[...] = _block_mask(vbd_scr.shape, GLA_CHUNK, GLA_DV).astype(F32).astype(BF16)

    @pl.when(first_of_seq)
    def _():
        k_scr[0:BLOCK, :] = jnp.zeros((BLOCK, k_scr.shape[1]), BF16)
        for kvh in range(N_KV_HEADS):
            vx_scr[0:BLOCK, 2 * kvh * 2 * HEAD_DIM:(2 * kvh + 1) * 2 * HEAD_DIM] = (
                jnp.zeros((BLOCK, 2 * HEAD_DIM), BF16))
        st_scr[...] = jnp.zeros(st_scr.shape, F32)

    @pl.when(jnp.logical_not(first_of_seq))
    def _():
        k_scr[0:BLOCK, :] = k_scr[rows:rows + BLOCK, :]
        vx_scr[0:BLOCK, :] = vx_scr[rows:rows + BLOCK, :]

    half_rows = rows // 2
    halves = [slice(i * half_rows, (i + 1) * half_rows) for i in range(2)]
    for rs in halves:
        h_scr[rs, :] = _rmsnorm(x_ref[rs, :], pre_g_ref[...]).astype(BF16)
    h = h_scr[...]

    def attention_projection(rs):
        att = jnp.dot(h_scr[rs, :], w_att_ref[...], preferred_element_type=F32)
        q_scr[rs, :] = (att[:, :ATT_WIDTH] * (HEAD_DIM ** -0.5)).astype(BF16)
        new_rows = slice(BLOCK + rs.start, BLOCK + rs.stop)
        k = att[:, ATT_WIDTH:ATT_WIDTH + KV_WIDTH]
        v = att[:, ATT_WIDTH + KV_WIDTH:]
        low = lax.broadcasted_iota(jnp.int32, k.shape, 1) < HEAD_DIM
        k_swap = pltpu.roll(k, HEAD_DIM, 1)
        v_swap = pltpu.roll(v, HEAD_DIM, 1)
        for kvh in range(N_KV_HEADS):
            keep = low if kvh == 0 else jnp.logical_not(low)
            k_scr[new_rows, kvh * 2 * HEAD_DIM:(kvh + 1) * 2 * HEAD_DIM] = (
                jnp.where(keep, k, k_swap).astype(BF16))
            vx_scr[new_rows, 2 * kvh * 2 * HEAD_DIM:(2 * kvh + 1) * 2 * HEAD_DIM] = (
                jnp.where(keep, v, v_swap).astype(BF16))

    gate_pieces = iter(range(N_GATE_PIECES))

    def gate_piece():
        i = next(gate_pieces)
        gc = slice(i * GATE_PIECE_COLS, (i + 1) * GATE_PIECE_COLS)
        sg_scr[:, gc] = _sigmoid(jnp.dot(h, w_gate_ref[:, gc], preferred_element_type=F32))

    a_lr = jnp.dot(h, w_alr_ref[...], preferred_element_type=F32).astype(BF16)
    attention_projection(halves[0])
    z = jnp.dot(a_lr, w_alpha_ref[...], preferred_element_type=F32) + b_alpha_ref[...]
    attention_projection(halves[1])
    log_a = _log_sigmoid(z) / GLA_TAU
    la_hi = log_a.astype(BF16)
    la_lo = (log_a - la_hi.astype(F32)).astype(BF16)
    tril = tril_scr[...]
    bcum = jnp.concatenate(
        [jnp.dot(tril, la_hi[g0:g0 + TRIL_ROWS, :], preferred_element_type=F32)
         + jnp.dot(tril, la_lo[g0:g0 + TRIL_ROWS, :], preferred_element_type=F32)
         for g0 in range(0, rows, TRIL_ROWS)], axis=0)

    pair_shape = (BLOCK, 2 * HEAD_DIM)
    low_half = lax.broadcasted_iota(jnp.int32, pair_shape, 1) < HEAD_DIM
    band_col = lax.broadcasted_iota(jnp.int32, (BLOCK, 2 * BLOCK), 1)
    no_prev_block = jnp.logical_and(first_of_seq,
                                    jnp.logical_and(band_col > 0, band_col < BLOCK))
    top = BF16_SUBLANES
    row0 = lax.broadcasted_iota(jnp.int32, (top, 2 * HEAD_DIM), 0) == 0

    def band(ref, n, c0):
        r0 = n * BLOCK
        head = jnp.where(row0, jnp.zeros((), BF16), ref[r0:r0 + top, c0:c0 + 2 * HEAD_DIM])
        return jnp.concatenate([head, ref[r0 + top:r0 + 2 * BLOCK, c0:c0 + 2 * HEAD_DIM]], axis=0)

    def attention_scores(n):
        scores = []
        for hd in range(N_Q_HEADS):
            pair = hd // 2
            q_pair = q_scr[n * BLOCK:(n + 1) * BLOCK, pair * 2 * HEAD_DIM:(pair + 1) * 2 * HEAD_DIM]
            sel = low_half if hd % 2 == 0 else jnp.logical_not(low_half)
            qm = jnp.where(sel, q_pair, jnp.zeros_like(q_pair))
            kvh = hd // (N_Q_HEADS // N_KV_HEADS)
            scores.append(_dot_nt(qm, band(k_scr, n, kvh * 2 * HEAD_DIM)))
        return scores

    def attention_outputs(n, scores):
        outs = []
        for hd, s in enumerate(scores):
            bm = bias_scr[hd]
            if n == 0:
                bm = jnp.where(no_prev_block, NEG_INF, bm)
            s = s + bm
            p = jnp.exp(s - jnp.max(s, axis=-1, keepdims=True)).astype(BF16)
            kvh = hd // (N_Q_HEADS // N_KV_HEADS)
            v_ext = jnp.concatenate(
                [band(vx_scr, n, 2 * kvh * 2 * HEAD_DIM),
                 vx_scr[n * BLOCK:(n + 2) * BLOCK,
                        (2 * kvh + 1) * 2 * HEAD_DIM:(2 * kvh + 2) * 2 * HEAD_DIM]], axis=1)
            o_ext = jnp.dot(p, v_ext, preferred_element_type=F32)
            outs.append(o_ext[:, :2 * HEAD_DIM] / o_ext[:, 2 * HEAD_DIM:])
        for pair in range(N_Q_HEADS // 2):
            ya_scr[n * BLOCK:(n + 1) * BLOCK, pair * 2 * HEAD_DIM:(pair + 1) * 2 * HEAD_DIM] = (
                jnp.where(low_half, outs[2 * pair], outs[2 * pair + 1]).astype(BF16))

    scores = attention_scores(0)
    qk = jnp.dot(h, w_gla_ref[:, :2 * GLA_K_WIDTH], preferred_element_type=F32)
    attention_outputs(0, scores)
    scores = attention_scores(1)
    v_all = jnp.dot(h, w_gla_ref[:, 2 * GLA_K_WIDTH:2 * GLA_K_WIDTH + GLA_V_WIDTH],
                    preferred_element_type=F32).astype(BF16)
    attention_outputs(1, scores)
    q_all = qk[:, :GLA_K_WIDTH] * (GLA_DK ** -0.5)
    k_all = qk[:, GLA_K_WIDTH:]
    q_dec_all = (q_all * jnp.exp(bcum)).astype(BF16)
    k_intra_all = (k_all * jnp.exp(-bcum)).astype(BF16)
    for n in range(2, rows // BLOCK):
        scores = attention_scores(n)
        gate_piece()
        attention_outputs(n, scores)

    chunks = [slice(c * GLA_CHUNK, (c + 1) * GLA_CHUNK) for c in range(rows // GLA_CHUNK)]
    kbd_mask = kbd_scr[...] != 0
    vbd_mask = vbd_scr[...] != 0
    causal = (lax.broadcasted_iota(jnp.int32, (GLA_CHUNK, GLA_K_WIDTH), 0)
              >= lax.broadcasted_iota(jnp.int32, (GLA_CHUNK, GLA_K_WIDTH), 1) % GLA_CHUNK)
    attn, d_state, decay_col = [], [], []
    for cs in chunks:
        k_bd = jnp.where(kbd_mask, jnp.concatenate([k_intra_all[cs, :]] * GLA_HEADS, axis=0),
                         jnp.zeros((), BF16))
        attn.append(jnp.where(causal, _dot_nt(q_dec_all[cs, :], k_bd), 0.0).astype(BF16))
        b_c = bcum[cs, :]
        b_last = b_c[GLA_CHUNK - 1:GLA_CHUNK, :]
        k_state_t = (k_all[cs, :] * jnp.exp(b_last - b_c)).T.astype(BF16)
        d_state.append([jnp.dot(k_state_t[hh * GLA_DK:(hh + 1) * GLA_DK, :],
                                v_all[cs, hh * GLA_DV:(hh + 1) * GLA_DV],
                                preferred_element_type=F32) for hh in range(GLA_HEADS)])
        decay = jnp.exp(b_last)
        decay_col.append(jnp.concatenate(
            [jnp.broadcast_to(decay[:, l0:l0 + 128], (128, 128)).T
             for l0 in range(0, GLA_K_WIDTH, 128)], axis=0))
    gate_piece()

    zero_blk = jnp.zeros((GLA_DK, GLA_DV), BF16)
    state = [st_scr[hh * GLA_DK:(hh + 1) * GLA_DK, :] for hh in range(GLA_HEADS)]
    for c, cs in enumerate(chunks):
        outs = []
        for pair in range(GLA_HEADS // 2):
            kl = slice(pair * 2 * GLA_DK, (pair + 1) * 2 * GLA_DK)
            v_pair = v_all[cs, pair * 2 * GLA_DV:(pair + 1) * 2 * GLA_DV]
            v_bd = jnp.where(vbd_mask, jnp.concatenate([v_pair] * 2, axis=0), jnp.zeros((), BF16))
            s_bd = jnp.concatenate(
                [jnp.concatenate([state[2 * pair].astype(BF16), zero_blk], axis=1),
                 jnp.concatenate([zero_blk, state[2 * pair + 1].astype(BF16)], axis=1)], axis=0)
            lhs = jnp.concatenate([attn[c][:, kl], q_dec_all[cs, kl]], axis=1)
            outs.append(jnp.dot(lhs, jnp.concatenate([v_bd, s_bd], axis=0),
                                preferred_element_type=F32))
        og_scr[cs, :] = jnp.concatenate(outs, axis=1)
        state = [state[hh] * decay_col[c][hh * GLA_DK:(hh + 1) * GLA_DK, :] + d_state[c][hh]
                 for hh in range(GLA_HEADS)]
    for hh in range(GLA_HEADS):
        st_scr[hh * GLA_DK:(hh + 1) * GLA_DK, :] = state[hh]

    r_all = jnp.dot(h, w_gla_ref[:, 2 * GLA_K_WIDTH + GLA_V_WIDTH:], preferred_element_type=F32)
    gate_piece()
    assert next(gate_pieces, None) is None
    m_a = jnp.dot(ya_scr[...], w_pa_ref[...], preferred_element_type=F32)
    gla_g = gla_g_ref[...]
    for hh in range(GLA_HEADS):
        lanes = slice(hh * GLA_DV, (hh + 1) * GLA_DV)
        r_h = r_all[:, lanes]
        yb_scr[:, lanes] = (_rmsnorm(og_scr[:, lanes], gla_g) * (r_h * _sigmoid(r_h))).astype(BF16)

    m_b = [jnp.dot(yb_scr[rs, :], w_pb_ref[...], preferred_element_type=F32) for rs in halves]
    mixed = []
    for i, rs in enumerate(halves):
        merged = (sg_scr[rs, :D_MODEL] * m_a[rs, :] + sg_scr[rs, D_MODEL:] * m_b[i]).astype(BF16)
        mixed.append(jnp.dot(merged, w_out_ref[...], preferred_element_type=F32))
    for i, rs in enumerate(halves):
        o_ref[rs, :] = x_ref[rs, :] + _rmsnorm(mixed[i], post_g_ref[...])


def _t5_bucket_table():
    t = np.arange(BLOCK)[:, None]
    j = np.arange(2 * BLOCK)[None, :]
    n = np.clip(t + BLOCK - j, 0, REL_MAX_DIST - 1)
    max_exact = REL_BUCKETS // 2
    nf = np.maximum(n, 1).astype(np.float32)
    large = max_exact + (np.log(nf / np.float32(max_exact))
                         / np.float32(math.log(REL_MAX_DIST / max_exact))
                         * np.float32(REL_BUCKETS - max_exact)).astype(np.int32)
    large = np.minimum(large, REL_BUCKETS - 1)
    return np.where(n < max_exact, n, large).astype(np.int32)


def _mixer(x2d, seq_len, rel_bias, pre_g, w_in, w_alpha, b_alpha, sinks, gla_g,
           w_proj_a, w_proj_b, w_out, post_g):
    n_rows = x2d.shape[0]
    offs = np.concatenate([[0], np.cumsum(IN_SPLITS)])
    o_q, o_qb, o_alr, o_ga, o_end = (int(offs[i]) for i in (0, 3, 7, 8, 10))
    w_att = w_in[:, o_q:o_qb].astype(BF16)
    w_gla = w_in[:, o_qb:o_alr].astype(BF16)
    w_alr_p = jnp.pad(w_in[:, o_alr:o_ga], ((0, 0), (0, ALR_PAD - GLA_GATE_RANK))).astype(BF16)
    w_alpha_p = jnp.pad(w_alpha, ((0, ALR_PAD - GLA_GATE_RANK), (0, 0))).astype(BF16)
    w_gate = w_in[:, o_ga:o_end].astype(BF16)
    bucket = jnp.asarray(_t5_bucket_table())

    steps = seq_len // MIX_ROWS
    smem = pl.BlockSpec(memory_space=pltpu.SMEM)
    row_spec = pl.BlockSpec((MIX_ROWS, D_MODEL), lambda b, j: (b * steps + j, 0))
    return pl.pallas_call(
        _mixer_kernel,
        out_shape=jax.ShapeDtypeStruct((n_rows, D_MODEL), F32),
        grid=(n_rows // seq_len, steps),
        in_specs=[
            smem, smem, row_spec,
            _const_spec((1, D_MODEL)),
            _const_spec(w_att.shape), _const_spec(w_gla.shape), _const_spec(w_alr_p.shape),
            _const_spec(w_alpha_p.shape), _const_spec((1, GLA_K_WIDTH)),
            _const_spec(w_gate.shape), _const_spec((1, GLA_DV)),
            _const_spec((ATT_WIDTH, D_MODEL)), _const_spec((GLA_V_WIDTH, D_MODEL)),
            _const_spec((D_MODEL, D_MODEL)), _const_spec((1, D_MODEL)),
            _const_spec(bucket.shape),
        ],
        out_specs=row_spec,
        scratch_shapes=[
            pltpu.VMEM((MIX_ROWS + BLOCK, KV2_WIDTH), BF16),
            pltpu.VMEM((MIX_ROWS + BLOCK, 2 * KV2_WIDTH), BF16),
            pltpu.VMEM((GLA_K_WIDTH, GLA_DV), F32),
            pltpu.VMEM((N_Q_HEADS, BLOCK, 2 * BLOCK), F32),
            pltpu.VMEM((TRIL_ROWS, TRIL_ROWS), BF16),
            pltpu.VMEM((GLA_HEADS * GLA_CHUNK, GLA_K_WIDTH), BF16),
            pltpu.VMEM((2 * GLA_CHUNK, 2 * GLA_DV), BF16),
            pltpu.VMEM((MIX_ROWS, D_MODEL), BF16),
            pltpu.VMEM((MIX_ROWS, ATT_WIDTH), BF16),
            pltpu.VMEM((MIX_ROWS, ATT_WIDTH), BF16),
            pltpu.VMEM((MIX_ROWS, GLA_V_WIDTH), BF16),
            pltpu.VMEM((MIX_ROWS, GLA_V_WIDTH), F32),
            pltpu.VMEM((MIX_ROWS, 2 * D_MODEL), F32),
        ],
        compiler_params=pltpu.CompilerParams(
            dimension_semantics=("arbitrary", "arbitrary"),
            vmem_limit_bytes=VMEM_LIMIT_BYTES),
        name="mixer",
    )(rel_bias, sinks, x2d, pre_g.reshape(1, D_MODEL), w_att, w_gla, w_alr_p, w_alpha_p,
      b_alpha.reshape(1, GLA_K_WIDTH), w_gate, gla_g.reshape(1, GLA_DV),
      w_proj_a.astype(BF16), w_proj_b.astype(BF16), w_out.astype(BF16),
      post_g.reshape(1, D_MODEL), bucket)


def kernel(x, rel_bias, ffn1_pre_g, ffn1_w_gate, ffn1_w_up, ffn1_w_down, ffn1_post_g, mix_pre_g, w_in, w_alpha, b_alpha, attn_sinks, gla_norm_g, w_proj_a, w_proj_b, w_out, mix_post_g, ffn2_pre_g, ffn2_w_gate, ffn2_w_up, ffn2_w_down, ffn2_post_g):
    b, s, d = x.shape
    x2d = x.reshape(b * s, d)
    x2d = _ffn(x2d, ffn1_pre_g[0], ffn1_w_gate[0], ffn1_w_up[0], ffn1_w_down[0], ffn1_post_g[0])
    x2d = _mixer(x2d, s, rel_bias, mix_pre_g[0], w_in[0], w_alpha[0], b_alpha[0], attn_sinks[0],
                 gla_norm_g[0], w_proj_a[0], w_proj_b[0], w_out[0], mix_post_g[0])
    x2d = _ffn(x2d, ffn2_pre_g[0], ffn2_w_gate[0], ffn2_w_up[0], ffn2_w_down[0], ffn2_post_g[0])
    return x2d.reshape(b, s, d)
```

```python
import functools
import math
from typing import NamedTuple

import numpy as np
import jax
import jax.numpy as jnp
from jax import lax
from jax.experimental import pallas as pl
from jax.experimental.pallas import tpu as pltpu

D_MODEL = 1024
D_FF = 2816
EPS = 1e-6

N_Q_HEADS = 8
N_KV_HEADS = 2
HEAD_DIM = 64
WINDOW = 128
BLOCK = 128
ATT_WIDTH = N_Q_HEADS * HEAD_DIM
KV_WIDTH = N_KV_HEADS * HEAD_DIM
KV2_WIDTH = 2 * KV_WIDTH
BF16_SUBLANES = 16
LANES = 128
REL_BUCKETS = 32
REL_MAX_DIST = 128
NEG_INF = -1e30
GLA_HEADS = 4
GLA_DK = 64
GLA_DV = 128
GLA_K_WIDTH = GLA_HEADS * GLA_DK
GLA_V_WIDTH = GLA_HEADS * GLA_DV
GLA_GATE_RANK = 16
GLA_TAU = 16.0
GLA_CHUNK = 64
ALR_PAD = 128
IN_SPLITS = (ATT_WIDTH, KV_WIDTH, KV_WIDTH, GLA_K_WIDTH, GLA_K_WIDTH, GLA_V_WIDTH,
             GLA_V_WIDTH, GLA_GATE_RANK, D_MODEL, D_MODEL)
MIX_ROWS = 512
TRIL_ROWS = 256
GATE_PIECE_COLS = 512
N_GATE_PIECES = 2 * D_MODEL // GATE_PIECE_COLS

FF_CHUNK = 256
N_FF_CHUNKS = D_FF // FF_CHUNK
FFN_SUB_ROWS = 512
FFN_ROWS = 2 * FFN_SUB_ROWS

FFN_DOWN_CAST_ROWS = 128
VMEM_LIMIT_BYTES = 56 * 1024 * 1024

F32 = jnp.float32
BF16 = jnp.bfloat16


def _rmsnorm(x, g):
    ms = jnp.mean(x * x, axis=-1, keepdims=True)
    return x * lax.rsqrt(ms + EPS) * g


def _sigmoid(x):
    return 1.0 / (1.0 + jnp.exp(-x))


def _const_spec(shape):
    return pl.BlockSpec(shape, lambda *_: (0,) * len(shape),
                        pipeline_mode=pl.Buffered(1))


class _Cast(NamedTuple):
    src: jax.Array
    rows: int
    cuts: tuple


def _whole(w, n_steps, min_rows=BF16_SUBLANES):
    rows = max(w.shape[0] // n_steps, min_rows)
    assert w.shape[0] % rows == 0 and rows % BF16_SUBLANES == 0
    return _Cast(w, rows, ((0, w.shape[1]),))


def _cast_specs(casts):
    in_specs, out_specs, out_shapes = [], [], []
    for cast in casts:
        n_rows, n_cols = cast.src.shape
        last = n_rows // cast.rows - 1
        index = lambda i, last=last: (jnp.minimum(i, last), 0)
        in_specs.append(pl.BlockSpec((cast.rows, n_cols), index))
        for c0, c1 in cast.cuts:
            width = pl.cdiv(c1 - c0, LANES) * LANES
            out_specs.append(pl.BlockSpec((cast.rows, width), index))
            out_shapes.append(jax.ShapeDtypeStruct((n_rows, width), BF16))
    return in_specs, out_specs, out_shapes


def _emit_casts(all_cuts, src_refs, dst_refs):
    dst_refs = iter(dst_refs)
    for cuts, src in zip(all_cuts, src_refs):
        for c0, c1 in cuts:
            dst = next(dst_refs)
            if c1 - c0 == dst.shape[1]:
                dst[...] = src[:, c0:c1].astype(BF16)
            else:
                tile = src[:, c0:c0 + dst.shape[1]]
                lane = lax.broadcasted_iota(jnp.int32, tile.shape, 1)
                dst[...] = jnp.where(lane < c1 - c0, tile, 0.0).astype(BF16)


def _ffn_kernel(x_ref, pre_g_ref, wg_ref, wu_ref, wd_ref, post_g_ref, *refs, cast_cuts):
    cast_src, o_ref, cast_dst = refs[:len(cast_cuts)], refs[len(cast_cuts)], refs[len(cast_cuts) + 1:]
    _emit_casts(cast_cuts, cast_src, cast_dst)
    starts = range(0, x_ref.shape[0], FFN_SUB_ROWS)
    xn = [_rmsnorm(x_ref[r0:r0 + FFN_SUB_ROWS, :], pre_g_ref[...]).astype(BF16) for r0 in starts]
    acc = [jnp.zeros((FFN_SUB_ROWS, D_MODEL), F32) for _ in starts]
    for c in range(N_FF_CHUNKS):
        cols = slice(c * FF_CHUNK, (c + 1) * FF_CHUNK)
        for t, r0 in enumerate(starts):
            g = jnp.dot(xn[t], wg_ref[:, cols], preferred_element_type=F32)
            u = jnp.dot(xn[t], wu_ref[:, cols], preferred_element_type=F32)
            h = (g * _sigmoid(g) * u).astype(BF16)
            acc[t] = acc[t] + jnp.dot(h, wd_ref[cols, :], preferred_element_type=F32)
            if c == N_FF_CHUNKS - 1:
                o_ref[r0:r0 + FFN_SUB_ROWS, :] = (
                    x_ref[r0:r0 + FFN_SUB_ROWS, :] + 0.5 * _rmsnorm(acc[t], post_g_ref[...]))


def _ffn(x2d, pre_g, w_gate, w_up, w_down, post_g, casts=()):
    n_rows = x2d.shape[0]
    cast_in, cast_out, cast_shapes = _cast_specs(casts)
    out = pl.pallas_call(
        functools.partial(_ffn_kernel, cast_cuts=tuple(c.cuts for c in casts)),
        out_shape=[jax.ShapeDtypeStruct((n_rows, D_MODEL), F32)] + cast_shapes,
        grid=(n_rows // FFN_ROWS,),
        in_specs=[
            pl.BlockSpec((FFN_ROWS, D_MODEL), lambda i: (i, 0)),
            _const_spec((1, D_MODEL)),
            _const_spec((D_MODEL, D_FF)),
            _const_spec((D_MODEL, D_FF)),
            _const_spec((D_FF, D_MODEL)),
            _const_spec((1, D_MODEL)),
        ] + cast_in,
        out_specs=[pl.BlockSpec((FFN_ROWS, D_MODEL), lambda i: (i, 0))] + cast_out,
        compiler_params=pltpu.CompilerParams(
            dimension_semantics=("arbitrary",),
            vmem_limit_bytes=VMEM_LIMIT_BYTES),
        name="ffn",
    )(x2d, pre_g.reshape(1, D_MODEL), w_gate.astype(BF16), w_up.astype(BF16),
      w_down.astype(BF16), post_g.reshape(1, D_MODEL), *[c.src for c in casts])
    return out[0], out[1:]


def _dot_nt(a, b):
    return lax.dot_general(a, b, (((1,), (1,)), ((), ())), preferred_element_type=F32)


def _log_sigmoid(z):
    return jnp.minimum(z, 0.0) - jnp.log1p(jnp.exp(-jnp.abs(z)))


def _block_mask(shape, row_block, col_block):
    row = lax.broadcasted_iota(jnp.int32, shape, 0) // row_block
    col = lax.broadcasted_iota(jnp.int32, shape, 1) // col_block
    return row == col


def _mixer_kernel(rel_ref, sinks_ref, x_ref, pre_g_ref, w_att_ref, w_gla_ref, w_alr_ref,
                  w_alpha_ref, b_alpha_ref, w_gate_ref, gla_g_ref, w_pa_ref, w_pb_ref,
                  w_out_ref, post_g_ref, bucket_ref, o_ref,
                  k_scr, vx_scr, st_scr, bias_scr, tril_scr, kbd_scr, vbd_scr,
                  h_scr, q_scr, ya_scr, yb_scr, og_scr, sg_scr):
    rows = x_ref.shape[0]
    seq_step = pl.program_id(1)
    first_of_seq = seq_step == 0

    @pl.when(jnp.logical_and(pl.program_id(0) == 0, first_of_seq))
    def _():
        bucket = bucket_ref[...]
        shape = (BLOCK, 2 * BLOCK)
        dist = (lax.broadcasted_iota(jnp.int32, shape, 0) + BLOCK
                - lax.broadcasted_iota(jnp.int32, shape, 1))
        valid = jnp.logical_and(dist >= 0, dist < WINDOW)
        for hd in range(N_Q_HEADS):
            def body(bk, acc, hd=hd):
                return jnp.where(bucket == bk, rel_ref[bk, hd], acc)
            bias = lax.fori_loop(0, REL_BUCKETS, body, jnp.zeros(shape, F32))
            bias_scr[hd] = jnp.where(lax.broadcasted_iota(jnp.int32, shape, 1) == 0,
                                     sinks_ref[hd], jnp.where(valid, bias, NEG_INF))
        ones = jnp.ones((vx_scr.shape[0], 2 * HEAD_DIM), BF16)
        for kvh in range(N_KV_HEADS):
            vx_scr[:, (2 * kvh + 1) * 2 * HEAD_DIM:(2 * kvh + 2) * 2 * HEAD_DIM] = ones
        tshape = (TRIL_ROWS, TRIL_ROWS)
        trow = lax.broadcasted_iota(jnp.int32, tshape, 0)
        tcol = lax.broadcasted_iota(jnp.int32, tshape, 1)
        tril_scr[...] = jnp.logical_and(trow // GLA_CHUNK == tcol // GLA_CHUNK,
                                        tcol <= trow).astype(F32).astype(BF16)
        kbd_scr[...] = _block_mask(kbd_scr.shape, GLA_CHUNK, GLA_DK).astype(F32).astype(BF16)
        vbd_scr[...] = _block_mask(vbd_scr.shape, GLA_CHUNK, GLA_DV).astype(F32).astype(BF16)

    @pl.when(first_of_seq)
    def _():
        k_scr[0:BLOCK, :] = jnp.zeros((BLOCK, k_scr.shape[1]), BF16)
        for kvh in range(N_KV_HEADS):
            vx_scr[0:BLOCK, 2 * kvh * 2 * HEAD_DIM:(2 * kvh + 1) * 2 * HEAD_DIM] = (
                jnp.zeros((BLOCK, 2 * HEAD_DIM), BF16))
        st_scr[...] = jnp.zeros(st_scr.shape, F32)

    @pl.when(jnp.logical_not(first_of_seq))
    def _():
        k_scr[0:BLOCK, :] = k_scr[rows:rows + BLOCK, :]
        vx_scr[0:BLOCK, :] = vx_scr[rows:rows + BLOCK, :]

    half_rows = rows // 2
    halves = [slice(i * half_rows, (i + 1) * half_rows) for i in range(2)]
    for rs in halves:
        h_scr[rs, :] = _rmsnorm(x_ref[rs, :], pre_g_ref[...]).astype(BF16)
    h = h_scr[...]

    def attention_projection(rs):
        att = jnp.dot(h_scr[rs, :], w_att_ref[...], preferred_element_type=F32)
        q_scr[rs, :] = (att[:, :ATT_WIDTH] * (HEAD_DIM ** -0.5)).astype(BF16)
        new_rows = slice(BLOCK + rs.start, BLOCK + rs.stop)
        k = att[:, ATT_WIDTH:ATT_WIDTH + KV_WIDTH]
        v = att[:, ATT_WIDTH + KV_WIDTH:]
        low = lax.broadcasted_iota(jnp.int32, k.shape, 1) < HEAD_DIM
        k_swap = pltpu.roll(k, HEAD_DIM, 1)
        v_swap = pltpu.roll(v, HEAD_DIM, 1)
        for kvh in range(N_KV_HEADS):
            keep = low if kvh == 0 else jnp.logical_not(low)
            k_scr[new_rows, kvh * 2 * HEAD_DIM:(kvh + 1) * 2 * HEAD_DIM] = (
                jnp.where(keep, k, k_swap).astype(BF16))
            vx_scr[new_rows, 2 * kvh * 2 * HEAD_DIM:(2 * kvh + 1) * 2 * HEAD_DIM] = (
                jnp.where(keep, v, v_swap).astype(BF16))

    gate_pieces = iter(range(N_GATE_PIECES))

    def gate_piece():
        i = next(gate_pieces)
        gc = slice(i * GATE_PIECE_COLS, (i + 1) * GATE_PIECE_COLS)
        sg_scr[:, gc] = _sigmoid(jnp.dot(h, w_gate_ref[:, gc], preferred_element_type=F32))

    a_lr = jnp.dot(h, w_alr_ref[...], preferred_element_type=F32).astype(BF16)
    attention_projection(halves[0])
    z = jnp.dot(a_lr, w_alpha_ref[...], preferred_element_type=F32) + b_alpha_ref[...]
    attention_projection(halves[1])
    log_a = _log_sigmoid(z) / GLA_TAU
    la_hi = log_a.astype(BF16)
    la_lo = (log_a - la_hi.astype(F32)).astype(BF16)
    tril = tril_scr[...]
    bcum = jnp.concatenate(
        [jnp.dot(tril, la_hi[g0:g0 + TRIL_ROWS, :], preferred_element_type=F32)
         + jnp.dot(tril, la_lo[g0:g0 + TRIL_ROWS, :], preferred_element_type=F32)
         for g0 in range(0, rows, TRIL_ROWS)], axis=0)

    pair_shape = (BLOCK, 2 * HEAD_DIM)
    low_half = lax.broadcasted_iota(jnp.int32, pair_shape, 1) < HEAD_DIM
    band_col = lax.broadcasted_iota(jnp.int32, (BLOCK, 2 * BLOCK), 1)
    no_prev_block = jnp.logical_and(first_of_seq,
                                    jnp.logical_and(band_col > 0, band_col < BLOCK))
    top = BF16_SUBLANES
    row0 = lax.broadcasted_iota(jnp.int32, (top, 2 * HEAD_DIM), 0) == 0

    def band(ref, n, c0):
        r0 = n * BLOCK
        head = jnp.where(row0, jnp.zeros((), BF16), ref[r0:r0 + top, c0:c0 + 2 * HEAD_DIM])
        return jnp.concatenate([head, ref[r0 + top:r0 + 2 * BLOCK, c0:c0 + 2 * HEAD_DIM]], axis=0)

    group = N_Q_HEADS // N_KV_HEADS

    def attention_scores(n):
        scores = []
        for kvh in range(N_KV_HEADS):
            stacked = []
            for hd in range(kvh * group, (kvh + 1) * group):
                pair = hd // 2
                q_pair = q_scr[n * BLOCK:(n + 1) * BLOCK,
                               pair * 2 * HEAD_DIM:(pair + 1) * 2 * HEAD_DIM]
                sel = low_half if hd % 2 == 0 else jnp.logical_not(low_half)
                stacked.append(jnp.where(sel, q_pair, jnp.zeros_like(q_pair)))
            s_group = _dot_nt(jnp.concatenate(stacked, axis=0), band(k_scr, n, kvh * 2 * HEAD_DIM))
            scores += [s_group[i * BLOCK:(i + 1) * BLOCK, :] for i in range(group)]
        return scores

    def attention_outputs(n, scores):
        outs = []
        for kvh in range(N_KV_HEADS):
            probs = []
            for hd in range(kvh * group, (kvh + 1) * group):
                bm = bias_scr[hd]
                if n == 0:
                    bm = jnp.where(no_prev_block, NEG_INF, bm)
                s = scores[hd] + bm
                probs.append(jnp.exp(s - jnp.max(s, axis=-1, keepdims=True)).astype(BF16))
            v_ext = jnp.concatenate(
                [band(vx_scr, n, 2 * kvh * 2 * HEAD_DIM),
                 vx_scr[n * BLOCK:(n + 2) * BLOCK,
                        (2 * kvh + 1) * 2 * HEAD_DIM:(2 * kvh + 2) * 2 * HEAD_DIM]], axis=1)
            o_ext = jnp.dot(jnp.concatenate(probs, axis=0), v_ext, preferred_element_type=F32)
            for i in range(group):
                o_h = o_ext[i * BLOCK:(i + 1) * BLOCK, :]
                outs.append(o_h[:, :2 * HEAD_DIM] / o_h[:, 2 * HEAD_DIM:])
        for pair in range(N_Q_HEADS // 2):
            ya_scr[n * BLOCK:(n + 1) * BLOCK, pair * 2 * HEAD_DIM:(pair + 1) * 2 * HEAD_DIM] = (
                jnp.where(low_half, outs[2 * pair], outs[2 * pair + 1]).astype(BF16))

    scores = attention_scores(0)
    qk = jnp.dot(h, w_gla_ref[:, :2 * GLA_K_WIDTH], preferred_element_type=F32)
    attention_outputs(0, scores)
    scores = attention_scores(1)
    v_all = jnp.dot(h, w_gla_ref[:, 2 * GLA_K_WIDTH:2 * GLA_K_WIDTH + GLA_V_WIDTH],
                    preferred_element_type=F32).astype(BF16)
    attention_outputs(1, scores)
    q_all = qk[:, :GLA_K_WIDTH] * (GLA_DK ** -0.5)
    k_all = qk[:, GLA_K_WIDTH:]
    q_dec_all = (q_all * jnp.exp(bcum)).astype(BF16)
    k_intra_all = (k_all * jnp.exp(-bcum)).astype(BF16)
    for n in range(2, rows // BLOCK):
        scores = attention_scores(n)
        if n < N_GATE_PIECES:
            gate_piece()
        attention_outputs(n, scores)

    chunks = [slice(c * GLA_CHUNK, (c + 1) * GLA_CHUNK) for c in range(rows // GLA_CHUNK)]
    kbd_mask = kbd_scr[...] != 0
    vbd_mask = vbd_scr[...] != 0
    causal = (lax.broadcasted_iota(jnp.int32, (GLA_CHUNK, GLA_K_WIDTH), 0)
              >= lax.broadcasted_iota(jnp.int32, (GLA_CHUNK, GLA_K_WIDTH), 1) % GLA_CHUNK)
    attn, d_state, decay_col = [], [], []
    for cs in chunks:
        k_bd = jnp.where(kbd_mask, jnp.concatenate([k_intra_all[cs, :]] * GLA_HEADS, axis=0),
                         jnp.zeros((), BF16))
        attn.append(jnp.where(causal, _dot_nt(q_dec_all[cs, :], k_bd), 0.0).astype(BF16))
        b_c = bcum[cs, :]
        b_last = b_c[GLA_CHUNK - 1:GLA_CHUNK, :]
        k_state_t = (k_all[cs, :] * jnp.exp(b_last - b_c)).T.astype(BF16)
        d_state.append([jnp.dot(k_state_t[hh * GLA_DK:(hh + 1) * GLA_DK, :],
                                v_all[cs, hh * GLA_DV:(hh + 1) * GLA_DV],
                                preferred_element_type=F32) for hh in range(GLA_HEADS)])
        decay = jnp.exp(b_last)
        decay_col.append(jnp.concatenate(
            [jnp.broadcast_to(decay[:, l0:l0 + LANES], (LANES, LANES)).T
             for l0 in range(0, GLA_K_WIDTH, LANES)], axis=0))
    gate_piece()

    zero_blk = jnp.zeros((GLA_DK, GLA_DV), BF16)
    state = [st_scr[hh * GLA_DK:(hh + 1) * GLA_DK, :] for hh in range(GLA_HEADS)]
    for c, cs in enumerate(chunks):
        outs = []
        for pair in range(GLA_HEADS // 2):
            kl = slice(pair * 2 * GLA_DK, (pair + 1) * 2 * GLA_DK)
            v_pair = v_all[cs, pair * 2 * GLA_DV:(pair + 1) * 2 * GLA_DV]
            v_bd = jnp.where(vbd_mask, jnp.concatenate([v_pair] * 2, axis=0), jnp.zeros((), BF16))
            s_bd = jnp.concatenate(
                [jnp.concatenate([state[2 * pair].astype(BF16), zero_blk], axis=1),
                 jnp.concatenate([zero_blk, state[2 * pair + 1].astype(BF16)], axis=1)], axis=0)
            lhs = jnp.concatenate([attn[c][:, kl], q_dec_all[cs, kl]], axis=1)
            outs.append(jnp.dot(lhs, jnp.concatenate([v_bd, s_bd], axis=0),
                                preferred_element_type=F32))
        og_scr[cs, :] = jnp.concatenate(outs, axis=1)
        state = [state[hh] * decay_col[c][hh * GLA_DK:(hh + 1) * GLA_DK, :] + d_state[c][hh]
                 for hh in range(GLA_HEADS)]
    for hh in range(GLA_HEADS):
        st_scr[hh * GLA_DK:(hh + 1) * GLA_DK, :] = state[hh]

    r_all = jnp.dot(h, w_gla_ref[:, 2 * GLA_K_WIDTH + GLA_V_WIDTH:], preferred_element_type=F32)
    gate_piece()
    assert next(gate_pieces, None) is None
    m_a = jnp.dot(ya_scr[...], w_pa_ref[...], preferred_element_type=F32)
    gla_g = gla_g_ref[...]
    for hh in range(GLA_HEADS):
        lanes = slice(hh * GLA_DV, (hh + 1) * GLA_DV)
        r_h = r_all[:, lanes]
        yb_scr[:, lanes] = (_rmsnorm(og_scr[:, lanes], gla_g) * (r_h * _sigmoid(r_h))).astype(BF16)

    m_b = [jnp.dot(yb_scr[rs, :], w_pb_ref[...], preferred_element_type=F32) for rs in halves]
    mixed = []
    for i, rs in enumerate(halves):
        merged = (sg_scr[rs, :D_MODEL] * m_a[rs, :] + sg_scr[rs, D_MODEL:] * m_b[i]).astype(BF16)
        mixed.append(jnp.dot(merged, w_out_ref[...], preferred_element_type=F32))
    for i, rs in enumerate(halves):
        o_ref[rs, :] = x_ref[rs, :] + _rmsnorm(mixed[i], post_g_ref[...])


def _t5_bucket_table():
    t = np.arange(BLOCK)[:, None]
    j = np.arange(2 * BLOCK)[None, :]
    n = np.clip(t + BLOCK - j, 0, REL_MAX_DIST - 1)
    max_exact = REL_BUCKETS // 2
    nf = np.maximum(n, 1).astype(np.float32)
    large = max_exact + (np.log(nf / np.float32(max_exact))
                         / np.float32(math.log(REL_MAX_DIST / max_exact))
                         * np.float32(REL_BUCKETS - max_exact)).astype(np.int32)
    large = np.minimum(large, REL_BUCKETS - 1)
    return np.where(n < max_exact, n, large).astype(np.int32)


def _mixer_weight_casts(w_in, w_proj_a, w_proj_b, w_out, n_steps):
    offs = np.concatenate([[0], np.cumsum(IN_SPLITS)])
    o_q, o_qb, o_alr, o_ga, o_end = (int(offs[i]) for i in (0, 3, 7, 8, 10))
    cuts = ((o_q, o_qb), (o_qb, o_alr), (o_alr, o_ga), (o_ga, o_end))
    return [_Cast(w_in, w_in.shape[0] // n_steps, cuts)] + [
        _whole(w, n_steps) for w in (w_proj_a, w_proj_b, w_out)]


def _mixer(x2d, seq_len, rel_bias, pre_g, w_att, w_gla, w_alr_p, w_gate, w_alpha, b_alpha, sinks,
           gla_g, w_pa, w_pb, w_o, post_g):
    n_rows = x2d.shape[0]
    w_alpha_p = jnp.pad(w_alpha, ((0, ALR_PAD - GLA_GATE_RANK), (0, 0))).astype(BF16)
    bucket = jnp.asarray(_t5_bucket_table())

    steps = seq_len // MIX_ROWS
    smem = pl.BlockSpec(memory_space=pltpu.SMEM)
    row_spec = pl.BlockSpec((MIX_ROWS, D_MODEL), lambda b, j: (b * steps + j, 0))
    return pl.pallas_call(
        _mixer_kernel,
        out_shape=jax.ShapeDtypeStruct((n_rows, D_MODEL), F32),
        grid=(n_rows // seq_len, steps),
        in_specs=[
            smem, smem, row_spec,
            _const_spec((1, D_MODEL)),
            _const_spec(w_att.shape), _const_spec(w_gla.shape), _const_spec(w_alr_p.shape),
            _const_spec(w_alpha_p.shape), _const_spec((1, GLA_K_WIDTH)),
            _const_spec(w_gate.shape), _const_spec((1, GLA_DV)),
            _const_spec(w_pa.shape), _const_spec(w_pb.shape),
            _const_spec(w_o.shape), _const_spec((1, D_MODEL)),
            _const_spec(bucket.shape),
        ],
        out_specs=row_spec,
        scratch_shapes=[
            pltpu.VMEM((MIX_ROWS + BLOCK, KV2_WIDTH), BF16),
            pltpu.VMEM((MIX_ROWS + BLOCK, 2 * KV2_WIDTH), BF16),
            pltpu.VMEM((GLA_K_WIDTH, GLA_DV), F32),
            pltpu.VMEM((N_Q_HEADS, BLOCK, 2 * BLOCK), F32),
            pltpu.VMEM((TRIL_ROWS, TRIL_ROWS), BF16),
            pltpu.VMEM((GLA_HEADS * GLA_CHUNK, GLA_K_WIDTH), BF16),
            pltpu.VMEM((2 * GLA_CHUNK, 2 * GLA_DV), BF16),
            pltpu.VMEM((MIX_ROWS, D_MODEL), BF16),
            pltpu.VMEM((MIX_ROWS, ATT_WIDTH), BF16),
            pltpu.VMEM((MIX_ROWS, ATT_WIDTH), BF16),
            pltpu.VMEM((MIX_ROWS, GLA_V_WIDTH), BF16),
            pltpu.VMEM((MIX_ROWS, GLA_V_WIDTH), F32),
            pltpu.VMEM((MIX_ROWS, 2 * D_MODEL), F32),
        ],
        compiler_params=pltpu.CompilerParams(
            dimension_semantics=("arbitrary", "arbitrary"),
            vmem_limit_bytes=VMEM_LIMIT_BYTES),
        name="mixer",
    )(rel_bias, sinks, x2d, pre_g.reshape(1, D_MODEL), w_att, w_gla, w_alr_p, w_alpha_p,
      b_alpha.reshape(1, GLA_K_WIDTH), w_gate, gla_g.reshape(1, GLA_DV), w_pa, w_pb, w_o,
      post_g.reshape(1, D_MODEL), bucket)


def kernel(x, rel_bias, ffn1_pre_g, ffn1_w_gate, ffn1_w_up, ffn1_w_down, ffn1_post_g, mix_pre_g, w_in, w_alpha, b_alpha, attn_sinks, gla_norm_g, w_proj_a, w_proj_b, w_out, mix_post_g, ffn2_pre_g, ffn2_w_gate, ffn2_w_up, ffn2_w_down, ffn2_post_g):
    b, s, d = x.shape
    x2d = x.reshape(b * s, d)
    n_steps = x2d.shape[0] // FFN_ROWS
    casts = _mixer_weight_casts(w_in[0], w_proj_a[0], w_proj_b[0], w_out[0], n_steps) + [
        _whole(ffn2_w_gate[0], n_steps), _whole(ffn2_w_up[0], n_steps),
        _whole(ffn2_w_down[0], n_steps, min_rows=FFN_DOWN_CAST_ROWS)]
    x2d, (w_att, w_gla, w_alr_p, w_gate, w_pa, w_pb, w_o, *ffn2_w) = _ffn(
        x2d, ffn1_pre_g[0], ffn1_w_gate[0], ffn1_w_up[0], ffn1_w_down[0], ffn1_post_g[0], casts)
    x2d = _mixer(x2d, s, rel_bias, mix_pre_g[0], w_att, w_gla, w_alr_p, w_gate, w_alpha[0],
                 b_alpha[0], attn_sinks[0], gla_norm_g[0], w_pa, w_pb, w_o, mix_post_g[0])
    x2d, _ = _ffn(x2d, ffn2_pre_g[0], *ffn2_w, ffn2_post_g[0])
    return x2d.reshape(b, s, d)
```

```python
import functools
import math
from typing import NamedTuple

import numpy as np
import jax
import jax.numpy as jnp
from jax import lax
from jax.experimental import pallas as pl
from jax.experimental.pallas import tpu as pltpu

D_MODEL = 1024
D_FF = 2816
EPS = 1e-6

N_Q_HEADS = 8
N_KV_HEADS = 2
HEAD_DIM = 64
WINDOW = 128
BLOCK = 128
ATT_WIDTH = N_Q_HEADS * HEAD_DIM
KV_WIDTH = N_KV_HEADS * HEAD_DIM
KV2_WIDTH = 2 * KV_WIDTH
BF16_SUBLANES = 16
LANES = 128
REL_BUCKETS = 32
REL_MAX_DIST = 128
NEG_INF = -1e30
GLA_HEADS = 4
GLA_DK = 64
GLA_DV = 128
GLA_K_WIDTH = GLA_HEADS * GLA_DK
GLA_V_WIDTH = GLA_HEADS * GLA_DV
GLA_GATE_RANK = 16
GLA_TAU = 16.0
GLA_CHUNK = 64
ALR_PAD = 128
IN_SPLITS = (ATT_WIDTH, KV_WIDTH, KV_WIDTH, GLA_K_WIDTH, GLA_K_WIDTH, GLA_V_WIDTH,
             GLA_V_WIDTH, GLA_GATE_RANK, D_MODEL, D_MODEL)
MIX_ROWS = 512
TRIL_ROWS = 256
GATE_PIECE_COLS = 512
N_GATE_PIECES = 2 * D_MODEL // GATE_PIECE_COLS

FF_CHUNK = 256
N_FF_CHUNKS = D_FF // FF_CHUNK
FFN_SUB_ROWS = 512
FFN_ROWS = 2 * FFN_SUB_ROWS

FFN_DOWN_CAST_ROWS = 128
VMEM_LIMIT_BYTES = 56 * 1024 * 1024

F32 = jnp.float32
BF16 = jnp.bfloat16


def _rmsnorm(x, g):
    ms = jnp.mean(x * x, axis=-1, keepdims=True)
    return x * lax.rsqrt(ms + EPS) * g


def _sigmoid(x):
    return 1.0 / (1.0 + jnp.exp(-x))


def _const_spec(shape):
    return pl.BlockSpec(shape, lambda *_: (0,) * len(shape),
                        pipeline_mode=pl.Buffered(1))


class _Cast(NamedTuple):
    src: jax.Array
    rows: int
    cuts: tuple


def _whole(w, n_steps, min_rows=BF16_SUBLANES):
    rows = max(w.shape[0] // n_steps, min_rows)
    assert w.shape[0] % rows == 0 and rows % BF16_SUBLANES == 0
    return _Cast(w, rows, ((0, w.shape[1]),))


def _cast_specs(casts):
    in_specs, out_specs, out_shapes = [], [], []
    for cast in casts:
        n_rows, n_cols = cast.src.shape
        last = n_rows // cast.rows - 1
        index = lambda i, last=last: (jnp.minimum(i, last), 0)
        in_specs.append(pl.BlockSpec((cast.rows, n_cols), index))
        for c0, c1 in cast.cuts:
            width = pl.cdiv(c1 - c0, LANES) * LANES
            out_specs.append(pl.BlockSpec((cast.rows, width), index))
            out_shapes.append(jax.ShapeDtypeStruct((n_rows, width), BF16))
    return in_specs, out_specs, out_shapes


def _emit_casts(all_cuts, src_refs, dst_refs):
    dst_refs = iter(dst_refs)
    for cuts, src in zip(all_cuts, src_refs):
        for c0, c1 in cuts:
            dst = next(dst_refs)
            if c1 - c0 == dst.shape[1]:
                dst[...] = src[:, c0:c1].astype(BF16)
            else:
                tile = src[:, c0:c0 + dst.shape[1]]
                lane = lax.broadcasted_iota(jnp.int32, tile.shape, 1)
                dst[...] = jnp.where(lane < c1 - c0, tile, 0.0).astype(BF16)


def _ffn_kernel(x_ref, pre_g_ref, wg_ref, wu_ref, wd_ref, post_g_ref, *refs, cast_cuts):
    cast_src, o_ref, cast_dst = refs[:len(cast_cuts)], refs[len(cast_cuts)], refs[len(cast_cuts) + 1:]
    _emit_casts(cast_cuts, cast_src, cast_dst)
    starts = range(0, x_ref.shape[0], FFN_SUB_ROWS)
    xn = [_rmsnorm(x_ref[r0:r0 + FFN_SUB_ROWS, :], pre_g_ref[...]).astype(BF16) for r0 in starts]
    acc = [jnp.zeros((FFN_SUB_ROWS, D_MODEL), F32) for _ in starts]
    for c in range(N_FF_CHUNKS):
        cols = slice(c * FF_CHUNK, (c + 1) * FF_CHUNK)
        for t, r0 in enumerate(starts):
            g = jnp.dot(xn[t], wg_ref[:, cols], preferred_element_type=F32)
            u = jnp.dot(xn[t], wu_ref[:, cols], preferred_element_type=F32)
            h = (g * _sigmoid(g) * u).astype(BF16)
            acc[t] = acc[t] + jnp.dot(h, wd_ref[cols, :], preferred_element_type=F32)
            if c == N_FF_CHUNKS - 1:
                o_ref[r0:r0 + FFN_SUB_ROWS, :] = (
                    x_ref[r0:r0 + FFN_SUB_ROWS, :] + 0.5 * _rmsnorm(acc[t], post_g_ref[...]))


def _ffn(x2d, pre_g, w_gate, w_up, w_down, post_g, casts=()):
    n_rows = x2d.shape[0]
    cast_in, cast_out, cast_shapes = _cast_specs(casts)
    out = pl.pallas_call(
        functools.partial(_ffn_kernel, cast_cuts=tuple(c.cuts for c in casts)),
        out_shape=[jax.ShapeDtypeStruct((n_rows, D_MODEL), F32)] + cast_shapes,
        grid=(n_rows // FFN_ROWS,),
        in_specs=[
            pl.BlockSpec((FFN_ROWS, D_MODEL), lambda i: (i, 0)),
            _const_spec((1, D_MODEL)),
            _const_spec((D_MODEL, D_FF)),
            _const_spec((D_MODEL, D_FF)),
            _const_spec((D_FF, D_MODEL)),
            _const_spec((1, D_MODEL)),
        ] + cast_in,
        out_specs=[pl.BlockSpec((FFN_ROWS, D_MODEL), lambda i: (i, 0))] + cast_out,
        compiler_params=pltpu.CompilerParams(
            dimension_semantics=("arbitrary",),
            vmem_limit_bytes=VMEM_LIMIT_BYTES),
        name="ffn",
    )(x2d, pre_g.reshape(1, D_MODEL), w_gate.astype(BF16), w_up.astype(BF16),
      w_down.astype(BF16), post_g.reshape(1, D_MODEL), *[c.src for c in casts])
    return out[0], out[1:]


def _dot_nt(a, b):
    return lax.dot_general(a, b, (((1,), (1,)), ((), ())), preferred_element_type=F32)


def _log_sigmoid(z):
    return jnp.minimum(z, 0.0) - jnp.log1p(jnp.exp(-jnp.abs(z)))


def _block_mask(shape, row_block, col_block):
    row = lax.broadcasted_iota(jnp.int32, shape, 0) // row_block
    col = lax.broadcasted_iota(jnp.int32, shape, 1) // col_block
    return row == col


def _mixer_kernel(rel_ref, sinks_ref, x_ref, pre_g_ref, w_att_ref, w_gla_ref, w_alr_ref,
                  w_alpha_ref, b_alpha_ref, w_gate_ref, gla_g_ref, w_pa_ref, w_pb_ref,
                  w_out_ref, post_g_ref, bucket_ref, o_ref,
                  k_scr, vx_scr, st_scr, bias_scr, tril_scr, kbd_scr, vbd_scr,
                  h_scr, q_scr, ya_scr, yb_scr, og_scr, sg_scr):
    rows = x_ref.shape[0]
    seq_step = pl.program_id(1)
    first_of_seq = seq_step == 0

    @pl.when(jnp.logical_and(pl.program_id(0) == 0, first_of_seq))
    def _():
        bucket = bucket_ref[...]
        shape = (BLOCK, 2 * BLOCK)
        dist = (lax.broadcasted_iota(jnp.int32, shape, 0) + BLOCK
                - lax.broadcasted_iota(jnp.int32, shape, 1))
        valid = jnp.logical_and(dist >= 0, dist < WINDOW)
        for hd in range(N_Q_HEADS):
            def body(bk, acc, hd=hd):
                return jnp.where(bucket == bk, rel_ref[bk, hd], acc)
            bias = lax.fori_loop(0, REL_BUCKETS, body, jnp.zeros(shape, F32))
            bias_scr[hd] = jnp.where(lax.broadcasted_iota(jnp.int32, shape, 1) == 0,
                                     sinks_ref[hd], jnp.where(valid, bias, NEG_INF))
        ones = jnp.ones((vx_scr.shape[0], 2 * HEAD_DIM), BF16)
        for kvh in range(N_KV_HEADS):
            vx_scr[:, (2 * kvh + 1) * 2 * HEAD_DIM:(2 * kvh + 2) * 2 * HEAD_DIM] = ones
        tshape = (TRIL_ROWS, TRIL_ROWS)
        trow = lax.broadcasted_iota(jnp.int32, tshape, 0)
        tcol = lax.broadcasted_iota(jnp.int32, tshape, 1)
        tril_scr[...] = jnp.logical_and(trow // GLA_CHUNK == tcol // GLA_CHUNK,
                                        tcol <= trow).astype(F32).astype(BF16)
        kbd_scr[...] = _block_mask(kbd_scr.shape, GLA_CHUNK, GLA_DK).astype(F32).astype(BF16)
        vbd_scr[...] = _block_mask(vbd_scr.shape, GLA_CHUNK, GLA_DV).astype(F32).astype(BF16)

    @pl.when(first_of_seq)
    def _():
        k_scr[0:BLOCK, :] = jnp.zeros((BLOCK, k_scr.shape[1]), BF16)
        for kvh in range(N_KV_HEADS):
            vx_scr[0:BLOCK, 2 * kvh * 2 * HEAD_DIM:(2 * kvh + 1) * 2 * HEAD_DIM] = (
                jnp.zeros((BLOCK, 2 * HEAD_DIM), BF16))
        st_scr[...] = jnp.zeros(st_scr.shape, F32)

    @pl.when(jnp.logical_not(first_of_seq))
    def _():
        k_scr[0:BLOCK, :] = k_scr[rows:rows + BLOCK, :]
        vx_scr[0:BLOCK, :] = vx_scr[rows:rows + BLOCK, :]

    half_rows = rows // 2
    halves = [slice(i * half_rows, (i + 1) * half_rows) for i in range(2)]
    for rs in halves:
        h_scr[rs, :] = _rmsnorm(x_ref[rs, :], pre_g_ref[...]).astype(BF16)
    h = h_scr[...]

    def attention_projection(rs):
        att = jnp.dot(h_scr[rs, :], w_att_ref[...], preferred_element_type=F32)
        q_scr[rs, :] = (att[:, :ATT_WIDTH] * (HEAD_DIM ** -0.5)).astype(BF16)
        new_rows = slice(BLOCK + rs.start, BLOCK + rs.stop)
        k = att[:, ATT_WIDTH:ATT_WIDTH + KV_WIDTH]
        v = att[:, ATT_WIDTH + KV_WIDTH:]
        low = lax.broadcasted_iota(jnp.int32, k.shape, 1) < HEAD_DIM
        k_swap = pltpu.roll(k, HEAD_DIM, 1)
        v_swap = pltpu.roll(v, HEAD_DIM, 1)
        for kvh in range(N_KV_HEADS):
            keep = low if kvh == 0 else jnp.logical_not(low)
            k_scr[new_rows, kvh * 2 * HEAD_DIM:(kvh + 1) * 2 * HEAD_DIM] = (
                jnp.where(keep, k, k_swap).astype(BF16))
            vx_scr[new_rows, 2 * kvh * 2 * HEAD_DIM:(2 * kvh + 1) * 2 * HEAD_DIM] = (
                jnp.where(keep, v, v_swap).astype(BF16))

    gate_pieces = iter(range(N_GATE_PIECES))

    def gate_piece():
        i = next(gate_pieces)
        gc = slice(i * GATE_PIECE_COLS, (i + 1) * GATE_PIECE_COLS)
        sg_scr[:, gc] = _sigmoid(jnp.dot(h, w_gate_ref[:, gc], preferred_element_type=F32))

    a_lr = jnp.dot(h, w_alr_ref[...], preferred_element_type=F32).astype(BF16)
    attention_projection(halves[0])
    z = jnp.dot(a_lr, w_alpha_ref[...], preferred_element_type=F32) + b_alpha_ref[...]
    attention_projection(halves[1])
    log_a = _log_sigmoid(z) / GLA_TAU
    la_hi = log_a.astype(BF16)
    la_lo = (log_a - la_hi.astype(F32)).astype(BF16)
    tril = tril_scr[...]
    bcum = jnp.concatenate(
        [jnp.dot(tril, la_hi[g0:g0 + TRIL_ROWS, :], preferred_element_type=F32)
         + jnp.dot(tril, la_lo[g0:g0 + TRIL_ROWS, :], preferred_element_type=F32)
         for g0 in range(0, rows, TRIL_ROWS)], axis=0)

    pair_shape = (BLOCK, 2 * HEAD_DIM)
    low_half = lax.broadcasted_iota(jnp.int32, pair_shape, 1) < HEAD_DIM
    band_col = lax.broadcasted_iota(jnp.int32, (BLOCK, 2 * BLOCK), 1)
    no_prev_block = jnp.logical_and(first_of_seq,
                                    jnp.logical_and(band_col > 0, band_col < BLOCK))
    top = BF16_SUBLANES
    row0 = lax.broadcasted_iota(jnp.int32, (top, 2 * HEAD_DIM), 0) == 0

    def band(ref, n, c0):
        r0 = n * BLOCK
        head = jnp.where(row0, jnp.zeros((), BF16), ref[r0:r0 + top, c0:c0 + 2 * HEAD_DIM])
        return jnp.concatenate([head, ref[r0 + top:r0 + 2 * BLOCK, c0:c0 + 2 * HEAD_DIM]], axis=0)

    def attention_scores(n):
        scores = []
        for hd in range(N_Q_HEADS):
            pair = hd // 2
            q_pair = q_scr[n * BLOCK:(n + 1) * BLOCK, pair * 2 * HEAD_DIM:(pair + 1) * 2 * HEAD_DIM]
            sel = low_half if hd % 2 == 0 else jnp.logical_not(low_half)
            qm = jnp.where(sel, q_pair, jnp.zeros_like(q_pair))
            kvh = hd // (N_Q_HEADS // N_KV_HEADS)
            scores.append(_dot_nt(qm, band(k_scr, n, kvh * 2 * HEAD_DIM)))
        return scores

    def attention_outputs(n, scores):
        outs = []
        for hd, s in enumerate(scores):
            bm = bias_scr[hd]
            if n == 0:
                bm = jnp.where(no_prev_block, NEG_INF, bm)
            s = s + bm
            p = jnp.exp(s - jnp.max(s, axis=-1, keepdims=True)).astype(BF16)
            kvh = hd // (N_Q_HEADS // N_KV_HEADS)
            v_ext = jnp.concatenate(
                [band(vx_scr, n, 2 * kvh * 2 * HEAD_DIM),
                 vx_scr[n * BLOCK:(n + 2) * BLOCK,
                        (2 * kvh + 1) * 2 * HEAD_DIM:(2 * kvh + 2) * 2 * HEAD_DIM]], axis=1)
            o_ext = jnp.dot(p, v_ext, preferred_element_type=F32)
            outs.append(o_ext[:, :2 * HEAD_DIM] / o_ext[:, 2 * HEAD_DIM:])
        for pair in range(N_Q_HEADS // 2):
            ya_scr[n * BLOCK:(n + 1) * BLOCK, pair * 2 * HEAD_DIM:(pair + 1) * 2 * HEAD_DIM] = (
                jnp.where(low_half, outs[2 * pair], outs[2 * pair + 1]).astype(BF16))

    scores = attention_scores(0)
    qk = jnp.dot(h, w_gla_ref[:, :2 * GLA_K_WIDTH], preferred_element_type=F32)
    attention_outputs(0, scores)
    scores = attention_scores(1)
    v_all = jnp.dot(h, w_gla_ref[:, 2 * GLA_K_WIDTH:2 * GLA_K_WIDTH + GLA_V_WIDTH],
                    preferred_element_type=F32).astype(BF16)
    attention_outputs(1, scores)
    q_all = qk[:, :GLA_K_WIDTH] * (GLA_DK ** -0.5)
    k_all = qk[:, GLA_K_WIDTH:]
    q_dec_all = (q_all * jnp.exp(bcum)).astype(BF16)
    k_intra_all = (k_all * jnp.exp(-bcum)).astype(BF16)
    for n in range(2, rows // BLOCK):
        scores = attention_scores(n)
        if n < N_GATE_PIECES:
            gate_piece()
        attention_outputs(n, scores)

    chunks = [slice(c * GLA_CHUNK, (c + 1) * GLA_CHUNK) for c in range(rows // GLA_CHUNK)]
    kbd_mask = kbd_scr[...] != 0
    vbd_mask = vbd_scr[...] != 0
    causal = (lax.broadcasted_iota(jnp.int32, (GLA_CHUNK, GLA_K_WIDTH), 0)
              >= lax.broadcasted_iota(jnp.int32, (GLA_CHUNK, GLA_K_WIDTH), 1) % GLA_CHUNK)
    attn, d_state, decay_col = [], [], []
    for cs in chunks:
        k_bd = jnp.where(kbd_mask, jnp.concatenate([k_intra_all[cs, :]] * GLA_HEADS, axis=0),
                         jnp.zeros((), BF16))
        attn.append(jnp.where(causal, _dot_nt(q_dec_all[cs, :], k_bd), 0.0).astype(BF16))
        b_c = bcum[cs, :]
        b_last = b_c[GLA_CHUNK - 1:GLA_CHUNK, :]
        stacked_t = jnp.concatenate([k_all[cs, :] * jnp.exp(b_last - b_c),
                                     jnp.broadcast_to(b_last, (8, GLA_K_WIDTH))], axis=0).T
        k_state_t = stacked_t[:, :GLA_CHUNK].astype(BF16)
        d_state.append([jnp.dot(k_state_t[hh * GLA_DK:(hh + 1) * GLA_DK, :],
                                v_all[cs, hh * GLA_DV:(hh + 1) * GLA_DV],
                                preferred_element_type=F32) for hh in range(GLA_HEADS)])
        decay_col.append(jnp.broadcast_to(jnp.exp(stacked_t[:, GLA_CHUNK:GLA_CHUNK + 1]),
                                          (GLA_K_WIDTH, GLA_DV)))
    gate_piece()

    zero_blk = jnp.zeros((GLA_DK, GLA_DV), BF16)
    state = [st_scr[hh * GLA_DK:(hh + 1) * GLA_DK, :] for hh in range(GLA_HEADS)]
    for c, cs in enumerate(chunks):
        outs = []
        for pair in range(GLA_HEADS // 2):
            kl = slice(pair * 2 * GLA_DK, (pair + 1) * 2 * GLA_DK)
            v_pair = v_all[cs, pair * 2 * GLA_DV:(pair + 1) * 2 * GLA_DV]
            v_bd = jnp.where(vbd_mask, jnp.concatenate([v_pair] * 2, axis=0), jnp.zeros((), BF16))
            s_bd = jnp.concatenate(
                [jnp.concatenate([state[2 * pair].astype(BF16), zero_blk], axis=1),
                 jnp.concatenate([zero_blk, state[2 * pair + 1].astype(BF16)], axis=1)], axis=0)
            lhs = jnp.concatenate([attn[c][:, kl], q_dec_all[cs, kl]], axis=1)
            outs.append(jnp.dot(lhs, jnp.concatenate([v_bd, s_bd], axis=0),
                                preferred_element_type=F32))
        og_scr[cs, :] = jnp.concatenate(outs, axis=1)
        state = [state[hh] * decay_col[c][hh * GLA_DK:(hh + 1) * GLA_DK, :] + d_state[c][hh]
                 for hh in range(GLA_HEADS)]
    for hh in range(GLA_HEADS):
        st_scr[hh * GLA_DK:(hh + 1) * GLA_DK, :] = state[hh]

    r_all = jnp.dot(h, w_gla_ref[:, 2 * GLA_K_WIDTH + GLA_V_WIDTH:], preferred_element_type=F32)
    gate_piece()
    assert next(gate_pieces, None) is None
    m_a = jnp.dot(ya_scr[...], w_pa_ref[...], preferred_element_type=F32)
    gla_g = gla_g_ref[...]
    for hh in range(GLA_HEADS):
        lanes = slice(hh * GLA_DV, (hh + 1) * GLA_DV)
        r_h = r_all[:, lanes]
        yb_scr[:, lanes] = (_rmsnorm(og_scr[:, lanes], gla_g) * (r_h * _sigmoid(r_h))).astype(BF16)

    m_b = jnp.dot(yb_scr[...], w_pb_ref[...], preferred_element_type=F32)
    merged = (sg_scr[:, :D_MODEL] * m_a + sg_scr[:, D_MODEL:] * m_b).astype(BF16)
    mixed = jnp.dot(merged, w_out_ref[...], preferred_element_type=F32)
    for rs in halves:
        o_ref[rs, :] = x_ref[rs, :] + _rmsnorm(mixed[rs, :], post_g_ref[...])


def _t5_bucket_table():
    t = np.arange(BLOCK)[:, None]
    j = np.arange(2 * BLOCK)[None, :]
    n = np.clip(t + BLOCK - j, 0, REL_MAX_DIST - 1)
    max_exact = REL_BUCKETS // 2
    nf = np.maximum(n, 1).astype(np.float32)
    large = max_exact + (np.log(nf / np.float32(max_exact))
                         / np.float32(math.log(REL_MAX_DIST / max_exact))
                         * np.float32(REL_BUCKETS - max_exact)).astype(np.int32)
    large = np.minimum(large, REL_BUCKETS - 1)
    return np.where(n < max_exact, n, large).astype(np.int32)


def _mixer_weight_casts(w_in, w_proj_a, w_proj_b, w_out, n_steps):
    offs = np.concatenate([[0], np.cumsum(IN_SPLITS)])
    o_q, o_qb, o_alr, o_ga, o_end = (int(offs[i]) for i in (0, 3, 7, 8, 10))
    cuts = ((o_q, o_qb), (o_qb, o_alr), (o_alr, o_ga), (o_ga, o_end))
    return [_Cast(w_in, w_in.shape[0] // n_steps, cuts)] + [
        _whole(w, n_steps) for w in (w_proj_a, w_proj_b, w_out)]


def _mixer(x2d, seq_len, rel_bias, pre_g, w_att, w_gla, w_alr_p, w_gate, w_alpha, b_alpha, sinks,
           gla_g, w_pa, w_pb, w_o, post_g):
    n_rows = x2d.shape[0]
    w_alpha_p = jnp.pad(w_alpha, ((0, ALR_PAD - GLA_GATE_RANK), (0, 0))).astype(BF16)
    bucket = jnp.asarray(_t5_bucket_table())

    steps = seq_len // MIX_ROWS
    smem = pl.BlockSpec(memory_space=pltpu.SMEM)
    row_spec = pl.BlockSpec((MIX_ROWS, D_MODEL), lambda b, j: (b * steps + j, 0))
    return pl.pallas_call(
        _mixer_kernel,
        out_shape=jax.ShapeDtypeStruct((n_rows, D_MODEL), F32),
        grid=(n_rows // seq_len, steps),
        in_specs=[
            smem, smem, row_spec,
            _const_spec((1, D_MODEL)),
            _const_spec(w_att.shape), _const_spec(w_gla.shape), _const_spec(w_alr_p.shape),
            _const_spec(w_alpha_p.shape), _const_spec((1, GLA_K_WIDTH)),
            _const_spec(w_gate.shape), _const_spec((1, GLA_DV)),
            _const_spec(w_pa.shape), _const_spec(w_pb.shape),
            _const_spec(w_o.shape), _const_spec((1, D_MODEL)),
            _const_spec(bucket.shape),
        ],
        out_specs=row_spec,
        scratch_shapes=[
            pltpu.VMEM((MIX_ROWS + BLOCK, KV2_WIDTH), BF16),
            pltpu.VMEM((MIX_ROWS + BLOCK, 2 * KV2_WIDTH), BF16),
            pltpu.VMEM((GLA_K_WIDTH, GLA_DV), F32),
            pltpu.VMEM((N_Q_HEADS, BLOCK, 2 * BLOCK), F32),
            pltpu.VMEM((TRIL_ROWS, TRIL_ROWS), BF16),
            pltpu.VMEM((GLA_HEADS * GLA_CHUNK, GLA_K_WIDTH), BF16),
            pltpu.VMEM((2 * GLA_CHUNK, 2 * GLA_DV), BF16),
            pltpu.VMEM((MIX_ROWS, D_MODEL), BF16),
            pltpu.VMEM((MIX_ROWS, ATT_WIDTH), BF16),
            pltpu.VMEM((MIX_ROWS, ATT_WIDTH), BF16),
            pltpu.VMEM((MIX_ROWS, GLA_V_WIDTH), BF16),
            pltpu.VMEM((MIX_ROWS, GLA_V_WIDTH), F32),
            pltpu.VMEM((MIX_ROWS, 2 * D_MODEL), F32),
        ],
        compiler_params=pltpu.CompilerParams(
            dimension_semantics=("arbitrary", "arbitrary"),
            vmem_limit_bytes=VMEM_LIMIT_BYTES),
        name="mixer",
    )(rel_bias, sinks, x2d, pre_g.reshape(1, D_MODEL), w_att, w_gla, w_alr_p, w_alpha_p,
      b_alpha.reshape(1, GLA_K_WIDTH), w_gate, gla_g.reshape(1, GLA_DV), w_pa, w_pb, w_o,
      post_g.reshape(1, D_MODEL), bucket)


def kernel(x, rel_bias, ffn1_pre_g, ffn1_w_gate, ffn1_w_up, ffn1_w_down, ffn1_post_g, mix_pre_g, w_in, w_alpha, b_alpha, attn_sinks, gla_norm_g, w_proj_a, w_proj_b, w_out, mix_post_g, ffn2_pre_g, ffn2_w_gate, ffn2_w_up, ffn2_w_down, ffn2_post_g):
    b, s, d = x.shape
    x2d = x.reshape(b * s, d)
    n_steps = x2d.shape[0] // FFN_ROWS
    casts = _mixer_weight_casts(w_in[0], w_proj_a[0], w_proj_b[0], w_out[0], n_steps) + [
        _whole(ffn2_w_gate[0], n_steps), _whole(ffn2_w_up[0], n_steps),
        _whole(ffn2_w_down[0], n_steps, min_rows=FFN_DOWN_CAST_ROWS)]
    x2d, (w_att, w_gla, w_alr_p, w_gate, w_pa, w_pb, w_o, *ffn2_w) = _ffn(
        x2d, ffn1_pre_g[0], ffn1_w_gate[0], ffn1_w_up[0], ffn1_w_down[0], ffn1_post_g[0], casts)
    x2d = _mixer(x2d, s, rel_bias, mix_pre_g[0], w_att, w_gla, w_alr_p, w_gate, w_alpha[0],
                 b_alpha[0], attn_sinks[0], gla_norm_g[0], w_pa, w_pb, w_o, mix_post_g[0])
    x2d, _ = _ffn(x2d, ffn2_pre_g[0], *ffn2_w, ffn2_post_g[0])
    return x2d.reshape(b, s, d)
```

```python
import functools
import math
from typing import NamedTuple

import numpy as np
import jax
import jax.numpy as jnp
from jax import lax
from jax.experimental import pallas as pl
from jax.experimental.pallas import tpu as pltpu

D_MODEL = 1024
D_FF = 2816
EPS = 1e-6

N_Q_HEADS = 8
N_KV_HEADS = 2
HEAD_DIM = 64
WINDOW = 128
BLOCK = 128
ATT_WIDTH = N_Q_HEADS * HEAD_DIM
KV_WIDTH = N_KV_HEADS * HEAD_DIM
KV2_WIDTH = 2 * KV_WIDTH
BF16_SUBLANES = 16
LANES = 128
REL_BUCKETS = 32
REL_MAX_DIST = 128
NEG_INF = -1e30
GLA_HEADS = 4
GLA_DK = 64
GLA_DV = 128
GLA_K_WIDTH = GLA_HEADS * GLA_DK
GLA_V_WIDTH = GLA_HEADS * GLA_DV
GLA_GATE_RANK = 16
GLA_TAU = 16.0
GLA_CHUNK = 64
ALR_PAD = 128
IN_SPLITS = (ATT_WIDTH, KV_WIDTH, KV_WIDTH, GLA_K_WIDTH, GLA_K_WIDTH, GLA_V_WIDTH,
             GLA_V_WIDTH, GLA_GATE_RANK, D_MODEL, D_MODEL)
MIX_ROWS = 512
TRIL_ROWS = 256
GATE_PIECE_COLS = 512
N_GATE_PIECES = 2 * D_MODEL // GATE_PIECE_COLS

FF_CHUNK = 256
N_FF_CHUNKS = D_FF // FF_CHUNK
FFN_SUB_ROWS = 512
FFN_ROWS = 2 * FFN_SUB_ROWS
FFN_LAG = 2

FFN_DOWN_CAST_ROWS = 128
VMEM_LIMIT_BYTES = 56 * 1024 * 1024

F32 = jnp.float32
BF16 = jnp.bfloat16


def _rmsnorm(x, g):
    ms = jnp.mean(x * x, axis=-1, keepdims=True)
    return x * lax.rsqrt(ms + EPS) * g


def _sigmoid(x):
    return 1.0 / (1.0 + jnp.exp(-x))


def _const_spec(shape):
    return pl.BlockSpec(shape, lambda *_: (0,) * len(shape),
                        pipeline_mode=pl.Buffered(1))


class _Cast(NamedTuple):
    src: jax.Array
    rows: int
    cuts: tuple


def _whole(w, n_steps, min_rows=BF16_SUBLANES):
    rows = max(w.shape[0] // n_steps, min_rows)
    assert w.shape[0] % rows == 0 and rows % BF16_SUBLANES == 0
    return _Cast(w, rows, ((0, w.shape[1]),))


def _cast_specs(casts):
    in_specs, out_specs, out_shapes = [], [], []
    for cast in casts:
        n_rows, n_cols = cast.src.shape
        last = n_rows // cast.rows - 1
        index = lambda i, last=last: (jnp.minimum(i, last), 0)
        in_specs.append(pl.BlockSpec((cast.rows, n_cols), index))
        for c0, c1 in cast.cuts:
            width = pl.cdiv(c1 - c0, LANES) * LANES
            out_specs.append(pl.BlockSpec((cast.rows, width), index))
            out_shapes.append(jax.ShapeDtypeStruct((n_rows, width), BF16))
    return in_specs, out_specs, out_shapes


def _emit_casts(all_cuts, src_refs, dst_refs):
    dst_refs = iter(dst_refs)
    for cuts, src in zip(all_cuts, src_refs):
        for c0, c1 in cuts:
            dst = next(dst_refs)
            if c1 - c0 == dst.shape[1]:
                dst[...] = src[:, c0:c1].astype(BF16)
            else:
                tile = src[:, c0:c0 + dst.shape[1]]
                lane = lax.broadcasted_iota(jnp.int32, tile.shape, 1)
                dst[...] = jnp.where(lane < c1 - c0, tile, 0.0).astype(BF16)


def _ffn_kernel(x_ref, pre_g_ref, wg_ref, wu_ref, wd_ref, post_g_ref, *refs, cast_cuts):
    cast_src, o_ref, cast_dst = refs[:len(cast_cuts)], refs[len(cast_cuts)], refs[len(cast_cuts) + 1:]
    _emit_casts(cast_cuts, cast_src, cast_dst)
    starts = list(range(0, x_ref.shape[0], FFN_SUB_ROWS))
    xn = [None] * len(starts)
    acc = [None] * len(starts)

    def chunk(t, c):
        r0 = starts[t]
        if c == 0:
            xn[t] = _rmsnorm(x_ref[r0:r0 + FFN_SUB_ROWS, :], pre_g_ref[...]).astype(BF16)
        cols = slice(c * FF_CHUNK, (c + 1) * FF_CHUNK)
        g = jnp.dot(xn[t], wg_ref[:, cols], preferred_element_type=F32)
        u = jnp.dot(xn[t], wu_ref[:, cols], preferred_element_type=F32)
        h = (g * _sigmoid(g) * u).astype(BF16)
        d = jnp.dot(h, wd_ref[cols, :], preferred_element_type=F32)
        acc[t] = d if c == 0 else acc[t] + d
        if c == N_FF_CHUNKS - 1:
            o_ref[r0:r0 + FFN_SUB_ROWS, :] = (
                x_ref[r0:r0 + FFN_SUB_ROWS, :] + 0.5 * _rmsnorm(acc[t], post_g_ref[...]))

    for c in range(N_FF_CHUNKS + FFN_LAG):
        if c < N_FF_CHUNKS:
            chunk(0, c)
        if c >= FFN_LAG:
            chunk(1, c - FFN_LAG)


def _ffn(x2d, pre_g, w_gate, w_up, w_down, post_g, casts=()):
    n_rows = x2d.shape[0]
    cast_in, cast_out, cast_shapes = _cast_specs(casts)
    out = pl.pallas_call(
        functools.partial(_ffn_kernel, cast_cuts=tuple(c.cuts for c in casts)),
        out_shape=[jax.ShapeDtypeStruct((n_rows, D_MODEL), F32)] + cast_shapes,
        grid=(n_rows // FFN_ROWS,),
        in_specs=[
            pl.BlockSpec((FFN_ROWS, D_MODEL), lambda i: (i, 0)),
            _const_spec((1, D_MODEL)),
            _const_spec((D_MODEL, D_FF)),
            _const_spec((D_MODEL, D_FF)),
            _const_spec((D_FF, D_MODEL)),
            _const_spec((1, D_MODEL)),
        ] + cast_in,
        out_specs=[pl.BlockSpec((FFN_ROWS, D_MODEL), lambda i: (i, 0))] + cast_out,
        compiler_params=pltpu.CompilerParams(
            dimension_semantics=("arbitrary",),
            vmem_limit_bytes=VMEM_LIMIT_BYTES),
        name="ffn",
    )(x2d, pre_g.reshape(1, D_MODEL), w_gate.astype(BF16), w_up.astype(BF16),
      w_down.astype(BF16), post_g.reshape(1, D_MODEL), *[c.src for c in casts])
    return out[0], out[1:]


def _dot_nt(a, b):
    return lax.dot_general(a, b, (((1,), (1,)), ((), ())), preferred_element_type=F32)


def _log_sigmoid(z):
    return jnp.minimum(z, 0.0) - jnp.log1p(jnp.exp(-jnp.abs(z)))


def _block_mask(shape, row_block, col_block):
    row = lax.broadcasted_iota(jnp.int32, shape, 0) // row_block
    col = lax.broadcasted_iota(jnp.int32, shape, 1) // col_block
    return row == col


def _mixer_kernel(rel_ref, sinks_ref, x_ref, pre_g_ref, w_att_ref, w_gla_ref, w_alr_ref,
                  w_alpha_ref, b_alpha_ref, w_gate_ref, gla_g_ref, w_pa_ref, w_pb_ref,
                  w_out_ref, post_g_ref, bucket_ref, o_ref,
                  k_scr, vx_scr, st_scr, bias_scr, tril_scr, kbd_scr, vbd_scr,
                  h_scr, q_scr, ya_scr, yb_scr, og_scr, sg_scr):
    rows = x_ref.shape[0]
    seq_step = pl.program_id(1)
    first_of_seq = seq_step == 0

    @pl.when(jnp.logical_and(pl.program_id(0) == 0, first_of_seq))
    def _():
        bucket = bucket_ref[...]
        shape = (BLOCK, 2 * BLOCK)
        dist = (lax.broadcasted_iota(jnp.int32, shape, 0) + BLOCK
                - lax.broadcasted_iota(jnp.int32, shape, 1))
        valid = jnp.logical_and(dist >= 0, dist < WINDOW)
        for hd in range(N_Q_HEADS):
            def body(bk, acc, hd=hd):
                return jnp.where(bucket == bk, rel_ref[bk, hd], acc)
            bias = lax.fori_loop(0, REL_BUCKETS, body, jnp.zeros(shape, F32))
            bias_scr[hd] = jnp.where(lax.broadcasted_iota(jnp.int32, shape, 1) == 0,
                                     sinks_ref[hd], jnp.where(valid, bias, NEG_INF))
        ones = jnp.ones((vx_scr.shape[0], 2 * HEAD_DIM), BF16)
        for kvh in range(N_KV_HEADS):
            vx_scr[:, (2 * kvh + 1) * 2 * HEAD_DIM:(2 * kvh + 2) * 2 * HEAD_DIM] = ones
        tshape = (TRIL_ROWS, TRIL_ROWS)
        trow = lax.broadcasted_iota(jnp.int32, tshape, 0)
        tcol = lax.broadcasted_iota(jnp.int32, tshape, 1)
        tril_scr[...] = jnp.logical_and(trow // GLA_CHUNK == tcol // GLA_CHUNK,
                                        tcol <= trow).astype(F32).astype(BF16)
        kbd_scr[...] = _block_mask(kbd_scr.shape, GLA_CHUNK, GLA_DK).astype(F32).astype(BF16)
        vbd_scr[...] = _block_mask(vbd_scr.shape, GLA_CHUNK, GLA_DV).astype(F32).astype(BF16)

    @pl.when(first_of_seq)
    def _():
        k_scr[0:BLOCK, :] = jnp.zeros((BLOCK, k_scr.shape[1]), BF16)
        for kvh in range(N_KV_HEADS):
            vx_scr[0:BLOCK, 2 * kvh * 2 * HEAD_DIM:(2 * kvh + 1) * 2 * HEAD_DIM] = (
                jnp.zeros((BLOCK, 2 * HEAD_DIM), BF16))
        st_scr[...] = jnp.zeros(st_scr.shape, F32)

    @pl.when(jnp.logical_not(first_of_seq))
    def _():
        k_scr[0:BLOCK, :] = k_scr[rows:rows + BLOCK, :]
        vx_scr[0:BLOCK, :] = vx_scr[rows:rows + BLOCK, :]

    half_rows = rows // 2
    halves = [slice(i * half_rows, (i + 1) * half_rows) for i in range(2)]
    for rs in halves:
        h_scr[rs, :] = _rmsnorm(x_ref[rs, :], pre_g_ref[...]).astype(BF16)
    h = h_scr[...]

    def attention_projection(rs):
        att = jnp.dot(h_scr[rs, :], w_att_ref[...], preferred_element_type=F32)
        q_scr[rs, :] = (att[:, :ATT_WIDTH] * (HEAD_DIM ** -0.5)).astype(BF16)
        new_rows = slice(BLOCK + rs.start, BLOCK + rs.stop)
        k = att[:, ATT_WIDTH:ATT_WIDTH + KV_WIDTH]
        v = att[:, ATT_WIDTH + KV_WIDTH:]
        low = lax.broadcasted_iota(jnp.int32, k.shape, 1) < HEAD_DIM
        k_swap = pltpu.roll(k, HEAD_DIM, 1)
        v_swap = pltpu.roll(v, HEAD_DIM, 1)
        for kvh in range(N_KV_HEADS):
            keep = low if kvh == 0 else jnp.logical_not(low)
            k_scr[new_rows, kvh * 2 * HEAD_DIM:(kvh + 1) * 2 * HEAD_DIM] = (
                jnp.where(keep, k, k_swap).astype(BF16))
            vx_scr[new_rows, 2 * kvh * 2 * HEAD_DIM:(2 * kvh + 1) * 2 * HEAD_DIM] = (
                jnp.where(keep, v, v_swap).astype(BF16))

    gate_pieces = iter(range(N_GATE_PIECES))

    def gate_piece():
        i = next(gate_pieces)
        gc = slice(i * GATE_PIECE_COLS, (i + 1) * GATE_PIECE_COLS)
        sg_scr[:, gc] = _sigmoid(jnp.dot(h, w_gate_ref[:, gc], preferred_element_type=F32))

    a_lr = jnp.dot(h, w_alr_ref[...], preferred_element_type=F32).astype(BF16)
    attention_projection(halves[0])
    z = jnp.dot(a_lr, w_alpha_ref[...], preferred_element_type=F32) + b_alpha_ref[...]
    attention_projection(halves[1])
    log_a = _log_sigmoid(z) / GLA_TAU
    la_hi = log_a.astype(BF16)
    la_lo = (log_a - la_hi.astype(F32)).astype(BF16)
    tril = tril_scr[...]
    bcum = jnp.concatenate(
        [jnp.dot(tril, la_hi[g0:g0 + TRIL_ROWS, :], preferred_element_type=F32)
         + jnp.dot(tril, la_lo[g0:g0 + TRIL_ROWS, :], preferred_element_type=F32)
         for g0 in range(0, rows, TRIL_ROWS)], axis=0)

    pair_shape = (BLOCK, 2 * HEAD_DIM)
    low_half = lax.broadcasted_iota(jnp.int32, pair_shape, 1) < HEAD_DIM
    band_col = lax.broadcasted_iota(jnp.int32, (BLOCK, 2 * BLOCK), 1)
    no_prev_block = jnp.logical_and(first_of_seq,
                                    jnp.logical_and(band_col > 0, band_col < BLOCK))
    top = BF16_SUBLANES
    row0 = lax.broadcasted_iota(jnp.int32, (top, 2 * HEAD_DIM), 0) == 0

    def band(ref, n, c0):
        r0 = n * BLOCK
        head = jnp.where(row0, jnp.zeros((), BF16), ref[r0:r0 + top, c0:c0 + 2 * HEAD_DIM])
        return jnp.concatenate([head, ref[r0 + top:r0 + 2 * BLOCK, c0:c0 + 2 * HEAD_DIM]], axis=0)

    def attention_scores(n):
        scores = []
        for hd in range(N_Q_HEADS):
            pair = hd // 2
            q_pair = q_scr[n * BLOCK:(n + 1) * BLOCK, pair * 2 * HEAD_DIM:(pair + 1) * 2 * HEAD_DIM]
            sel = low_half if hd % 2 == 0 else jnp.logical_not(low_half)
            qm = jnp.where(sel, q_pair, jnp.zeros_like(q_pair))
            kvh = hd // (N_Q_HEADS // N_KV_HEADS)
            scores.append(_dot_nt(qm, band(k_scr, n, kvh * 2 * HEAD_DIM)))
        return scores

    def attention_outputs(n, scores):
        outs = []
        for hd, s in enumerate(scores):
            bm = bias_scr[hd]
            if n == 0:
                bm = jnp.where(no_prev_block, NEG_INF, bm)
            s = s + bm
            p = jnp.exp(s - jnp.max(s, axis=-1, keepdims=True)).astype(BF16)
            kvh = hd // (N_Q_HEADS // N_KV_HEADS)
            v_ext = jnp.concatenate(
                [band(vx_scr, n, 2 * kvh * 2 * HEAD_DIM),
                 vx_scr[n * BLOCK:(n + 2) * BLOCK,
                        (2 * kvh + 1) * 2 * HEAD_DIM:(2 * kvh + 2) * 2 * HEAD_DIM]], axis=1)
            o_ext = jnp.dot(p, v_ext, preferred_element_type=F32)
            outs.append(o_ext[:, :2 * HEAD_DIM] / o_ext[:, 2 * HEAD_DIM:])
        for pair in range(N_Q_HEADS // 2):
            ya_scr[n * BLOCK:(n + 1) * BLOCK, pair * 2 * HEAD_DIM:(pair + 1) * 2 * HEAD_DIM] = (
                jnp.where(low_half, outs[2 * pair], outs[2 * pair + 1]).astype(BF16))

    scores = attention_scores(0)
    qk = jnp.dot(h, w_gla_ref[:, :2 * GLA_K_WIDTH], preferred_element_type=F32)
    attention_outputs(0, scores)
    scores = attention_scores(1)
    v_all = jnp.dot(h, w_gla_ref[:, 2 * GLA_K_WIDTH:2 * GLA_K_WIDTH + GLA_V_WIDTH],
                    preferred_element_type=F32).astype(BF16)
    attention_outputs(1, scores)
    q_all = qk[:, :GLA_K_WIDTH] * (GLA_DK ** -0.5)
    k_all = qk[:, GLA_K_WIDTH:]
    q_dec_all = (q_all * jnp.exp(bcum)).astype(BF16)
    k_intra_all = (k_all * jnp.exp(-bcum)).astype(BF16)
    for n in range(2, rows // BLOCK):
        scores = attention_scores(n)
        if n < N_GATE_PIECES:
            gate_piece()
        attention_outputs(n, scores)

    chunks = [slice(c * GLA_CHUNK, (c + 1) * GLA_CHUNK) for c in range(rows // GLA_CHUNK)]
    kbd_mask = kbd_scr[...] != 0
    vbd_mask = vbd_scr[...] != 0
    causal = (lax.broadcasted_iota(jnp.int32, (GLA_CHUNK, GLA_K_WIDTH), 0)
              >= lax.broadcasted_iota(jnp.int32, (GLA_CHUNK, GLA_K_WIDTH), 1) % GLA_CHUNK)
    attn, d_state, decay_col = [], [], []
    for cs in chunks:
        k_bd = jnp.where(kbd_mask, jnp.concatenate([k_intra_all[cs, :]] * GLA_HEADS, axis=0),
                         jnp.zeros((), BF16))
        attn.append(jnp.where(causal, _dot_nt(q_dec_all[cs, :], k_bd), 0.0).astype(BF16))
        b_c = bcum[cs, :]
        b_last = b_c[GLA_CHUNK - 1:GLA_CHUNK, :]
        k_state_t = (k_all[cs, :] * jnp.exp(b_last - b_c)).T.astype(BF16)
        d_state.append([jnp.dot(k_state_t[hh * GLA_DK:(hh + 1) * GLA_DK, :],
                                v_all[cs, hh * GLA_DV:(hh + 1) * GLA_DV],
                                preferred_element_type=F32) for hh in range(GLA_HEADS)])
        decay = jnp.exp(b_last)
        decay_col.append(jnp.concatenate(
            [jnp.broadcast_to(decay[:, l0:l0 + LANES], (LANES, LANES)).T
             for l0 in range(0, GLA_K_WIDTH, LANES)], axis=0))
    gate_piece()

    zero_blk = jnp.zeros((GLA_DK, GLA_DV), BF16)
    state = [st_scr[hh * GLA_DK:(hh + 1) * GLA_DK, :] for hh in range(GLA_HEADS)]
    for c, cs in enumerate(chunks):
        outs = []
        for pair in range(GLA_HEADS // 2):
            kl = slice(pair * 2 * GLA_DK, (pair + 1) * 2 * GLA_DK)
            v_pair = v_all[cs, pair * 2 * GLA_DV:(pair + 1) * 2 * GLA_DV]
            v_bd = jnp.where(vbd_mask, jnp.concatenate([v_pair] * 2, axis=0), jnp.zeros((), BF16))
            s_bd = jnp.concatenate(
                [jnp.concatenate([state[2 * pair].astype(BF16), zero_blk], axis=1),
                 jnp.concatenate([zero_blk, state[2 * pair + 1].astype(BF16)], axis=1)], axis=0)
            lhs = jnp.concatenate([attn[c][:, kl], q_dec_all[cs, kl]], axis=1)
            outs.append(jnp.dot(lhs, jnp.concatenate([v_bd, s_bd], axis=0),
                                preferred_element_type=F32))
        og_scr[cs, :] = jnp.concatenate(outs, axis=1)
        state = [state[hh] * decay_col[c][hh * GLA_DK:(hh + 1) * GLA_DK, :] + d_state[c][hh]
                 for hh in range(GLA_HEADS)]
    for hh in range(GLA_HEADS):
        st_scr[hh * GLA_DK:(hh + 1) * GLA_DK, :] = state[hh]

    r_all = jnp.dot(h, w_gla_ref[:, 2 * GLA_K_WIDTH + GLA_V_WIDTH:], preferred_element_type=F32)
    gate_piece()
    assert next(gate_pieces, None) is None
    m_a = jnp.dot(ya_scr[...], w_pa_ref[...], preferred_element_type=F32)
    gla_g = gla_g_ref[...]
    for hh in range(GLA_HEADS):
        lanes = slice(hh * GLA_DV, (hh + 1) * GLA_DV)
        r_h = r_all[:, lanes]
        yb_scr[:, lanes] = (_rmsnorm(og_scr[:, lanes], gla_g) * (r_h * _sigmoid(r_h))).astype(BF16)

    m_b = jnp.dot(yb_scr[...], w_pb_ref[...], preferred_element_type=F32)
    merged = (sg_scr[:, :D_MODEL] * m_a + sg_scr[:, D_MODEL:] * m_b).astype(BF16)
    mixed = jnp.dot(merged, w_out_ref[...], preferred_element_type=F32)
    for rs in halves:
        o_ref[rs, :] = x_ref[rs, :] + _rmsnorm(mixed[rs, :], post_g_ref[...])


def _t5_bucket_table():
    t = np.arange(BLOCK)[:, None]
    j = np.arange(2 * BLOCK)[None, :]
    n = np.clip(t + BLOCK - j, 0, REL_MAX_DIST - 1)
    max_exact = REL_BUCKETS // 2
    nf = np.maximum(n, 1).astype(np.float32)
    large = max_exact + (np.log(nf / np.float32(max_exact))
                         / np.float32(math.log(REL_MAX_DIST / max_exact))
                         * np.float32(REL_BUCKETS - max_exact)).astype(np.int32)
    large = np.minimum(large, REL_BUCKETS - 1)
    return np.where(n < max_exact, n, large).astype(np.int32)


def _mixer_weight_casts(w_in, w_proj_a, w_proj_b, w_out, n_steps):
    offs = np.concatenate([[0], np.cumsum(IN_SPLITS)])
    o_q, o_qb, o_alr, o_ga, o_end = (int(offs[i]) for i in (0, 3, 7, 8, 10))
    cuts = ((o_q, o_qb), (o_qb, o_alr), (o_alr, o_ga), (o_ga, o_end))
    return [_Cast(w_in, w_in.shape[0] // n_steps, cuts)] + [
        _whole(w, n_steps) for w in (w_proj_a, w_proj_b, w_out)]


def _mixer(x2d, seq_len, rel_bias, pre_g, w_att, w_gla, w_alr_p, w_gate, w_alpha, b_alpha, sinks,
           gla_g, w_pa, w_pb, w_o, post_g):
    n_rows = x2d.shape[0]
    w_alpha_p = jnp.pad(w_alpha, ((0, ALR_PAD - GLA_GATE_RANK), (0, 0))).astype(BF16)
    bucket = jnp.asarray(_t5_bucket_table())

    steps = seq_len // MIX_ROWS
    smem = pl.BlockSpec(memory_space=pltpu.SMEM)
    row_spec = pl.BlockSpec((MIX_ROWS, D_MODEL), lambda b, j: (b * steps + j, 0))
    return pl.pallas_call(
        _mixer_kernel,
        out_shape=jax.ShapeDtypeStruct((n_rows, D_MODEL), F32),
        grid=(n_rows // seq_len, steps),
        in_specs=[
            smem, smem, row_spec,
            _const_spec((1, D_MODEL)),
            _const_spec(w_att.shape), _const_spec(w_gla.shape), _const_spec(w_alr_p.shape),
            _const_spec(w_alpha_p.shape), _const_spec((1, GLA_K_WIDTH)),
            _const_spec(w_gate.shape), _const_spec((1, GLA_DV)),
            _const_spec(w_pa.shape), _const_spec(w_pb.shape),
            _const_spec(w_o.shape), _const_spec((1, D_MODEL)),
            _const_spec(bucket.shape),
        ],
        out_specs=row_spec,
        scratch_shapes=[
            pltpu.VMEM((MIX_ROWS + BLOCK, KV2_WIDTH), BF16),
            pltpu.VMEM((MIX_ROWS + BLOCK, 2 * KV2_WIDTH), BF16),
            pltpu.VMEM((GLA_K_WIDTH, GLA_DV), F32),
            pltpu.VMEM((N_Q_HEADS, BLOCK, 2 * BLOCK), F32),
            pltpu.VMEM((TRIL_ROWS, TRIL_ROWS), BF16),
            pltpu.VMEM((GLA_HEADS * GLA_CHUNK, GLA_K_WIDTH), BF16),
            pltpu.VMEM((2 * GLA_CHUNK, 2 * GLA_DV), BF16),
            pltpu.VMEM((MIX_ROWS, D_MODEL), BF16),
            pltpu.VMEM((MIX_ROWS, ATT_WIDTH), BF16),
            pltpu.VMEM((MIX_ROWS, ATT_WIDTH), BF16),
            pltpu.VMEM((MIX_ROWS, GLA_V_WIDTH), BF16),
            pltpu.VMEM((MIX_ROWS, GLA_V_WIDTH), F32),
            pltpu.VMEM((MIX_ROWS, 2 * D_MODEL), F32),
        ],
        compiler_params=pltpu.CompilerParams(
            dimension_semantics=("arbitrary", "arbitrary"),
            vmem_limit_bytes=VMEM_LIMIT_BYTES),
        name="mixer",
    )(rel_bias, sinks, x2d, pre_g.reshape(1, D_MODEL), w_att, w_gla, w_alr_p, w_alpha_p,
      b_alpha.reshape(1, GLA_K_WIDTH), w_gate, gla_g.reshape(1, GLA_DV), w_pa, w_pb, w_o,
      post_g.reshape(1, D_MODEL), bucket)


def kernel(x, rel_bias, ffn1_pre_g, ffn1_w_gate, ffn1_w_up, ffn1_w_down, ffn1_post_g, mix_pre_g, w_in, w_alpha, b_alpha, attn_sinks, gla_norm_g, w_proj_a, w_proj_b, w_out, mix_post_g, ffn2_pre_g, ffn2_w_gate, ffn2_w_up, ffn2_w_down, ffn2_post_g):
    b, s, d = x.shape
    x2d = x.reshape(b * s, d)
    n_steps = x2d.shape[0] // FFN_ROWS
    casts = _mixer_weight_casts(w_in[0], w_proj_a[0], w_proj_b[0], w_out[0], n_steps) + [
        _whole(ffn2_w_gate[0], n_steps), _whole(ffn2_w_up[0], n_steps),
        _whole(ffn2_w_down[0], n_steps, min_rows=FFN_DOWN_CAST_ROWS)]
    x2d, (w_att, w_gla, w_alr_p, w_gate, w_pa, w_pb, w_o, *ffn2_w) = _ffn(
        x2d, ffn1_pre_g[0], ffn1_w_gate[0], ffn1_w_up[0], ffn1_w_down[0], ffn1_post_g[0], casts)
    x2d = _mixer(x2d, s, rel_bias, mix_pre_g[0], w_att, w_gla, w_alr_p, w_gate, w_alpha[0],
                 b_alpha[0], attn_sinks[0], gla_norm_g[0], w_pa, w_pb, w_o, mix_post_g[0])
    x2d, _ = _ffn(x2d, ffn2_pre_g[0], *ffn2_w, ffn2_post_g[0])
    return x2d.reshape(b, s, d)
```
